```python
import jax, jax.numpy as jnp
from jax import lax
import numpy as np

D_MODEL = 4096
BATCH = 4
SEQ = 2048
DEPTH = 2
DEC_BATCH = 8
DEC_SEQ = 4
PAST_LEN = 16384
PAGE_SIZE = 128

HEAD_DIM = 128
D_SB = D_MODEL // 4
H_SB = D_SB // HEAD_DIM
D_CONV = D_MODEL // 4
D_NSA = D_MODEL // 2
H_NSA = D_NSA // HEAD_DIM
H_KV = 4
G_NSA = H_NSA // H_KV
D_KV = H_KV * HEAD_DIM
CONV_W = 31
L_CMP = 32
CMP_STRIDE = 16
L_SEL = 64
N_SEL = 16
WINDOW = 512
Q_BLOCK = 128
SEL_Q_BLOCK = 32
N_GROUPS = 4
E_PER_GROUP = 4
N_EXPERTS = N_GROUPS * E_PER_GROUP
TOP_K_IN_GROUP = 2
D_EXPERT = D_MODEL // 8
ALPHA = (2.0 * DEPTH) ** 0.25
BETA = (8.0 * DEPTH) ** -0.25
EPS = 1e-5
NEG = -1e30
BIG = 1e9
PROJ_SPLITS = (D_SB, D_SB, D_SB, D_CONV, D_CONV, D_NSA) + (D_KV,) * 6 + (3 * H_NSA,)
D_IN = sum(PROJ_SPLITS)

kernel_name = 'hybrid_sb_conformer_nsa_hmoe_step'


def layer_norm(x, g, b):
    xf = x.astype(jnp.float32)
    mu = jnp.mean(xf, -1, keepdims=True)
    var = jnp.mean(jnp.square(xf - mu), -1, keepdims=True)
    return ((xf - mu) * lax.rsqrt(var + EPS) * g + b).astype(x.dtype)


def head_rms_norm(o, g):
    B, T, H, hd = o.shape
    of = o.astype(jnp.float32)
    of = of * lax.rsqrt(jnp.mean(of * of, -1, keepdims=True) + EPS)
    return (of.reshape(B, T, H * hd) * g).astype(o.dtype)


def gather_pages(pool, page_table):
    g = pool[page_table]
    return g.reshape((g.shape[0], g.shape[1] * g.shape[2]) + g.shape[3:])


def stick_breaking(q, k, v, q_pos, k_pos):
    z = jnp.einsum('bqhd,bkhd->bhqk', q, k, preferred_element_type=jnp.float32) * (HEAD_DIM ** -0.5)
    mask = k_pos[None, :] < q_pos[:, None]
    log_beta = jax.nn.log_sigmoid(z)
    log_1m = jnp.where(mask, log_beta - z, 0.0)
    after = lax.cumsum(log_1m, axis=3, reverse=True) - log_1m
    a = jnp.where(mask, jnp.exp(log_beta + after), 0.0)
    return jnp.einsum('bhqk,bkhd->bqhd', a.astype(v.dtype), v)


def stick_breaking_blocked(q, k, v, q_pos, k_pos):
    B, Tq, H, hd = q.shape
    if Tq <= Q_BLOCK or Tq % Q_BLOCK:
        return stick_breaking(q, k, v, q_pos, k_pos)
    nb = Tq // Q_BLOCK
    qb = q.reshape(B, nb, Q_BLOCK, H, hd).transpose(1, 0, 2, 3, 4)
    pb = q_pos.reshape(nb, Q_BLOCK)
    ob = lax.map(lambda a: stick_breaking(a[0], k, v, a[1], k_pos), (qb, pb))
    return ob.transpose(1, 0, 2, 3, 4).reshape(B, Tq, H, hd)


def conformer_conv(a, b, buf, w, bias, g, beta):
    u = a * jax.nn.sigmoid(b)
    up = jnp.concatenate([buf, u], axis=1)
    y = lax.conv_general_dilated(up, w[:, None, :], window_strides=(1,), padding='VALID',
                                 dimension_numbers=('NWC', 'WIO', 'NWC'),
                                 feature_group_count=up.shape[-1]) + bias
    y = jax.nn.silu(layer_norm(y, g, beta))
    return y, up[:, -(CONV_W - 1):]


def compress_rows(rows, w, pe):
    B, T, H, hd = rows.shape
    R = L_CMP // CMP_STRIDE
    nc = T // CMP_STRIDE
    n_cmp = nc - R + 1
    chunks = rows[:, :nc * CMP_STRIDE].reshape(B, nc, CMP_STRIDE, H, hd)
    w = w.reshape(R, CMP_STRIDE, hd, hd)
    pe = pe.reshape(R, CMP_STRIDE, hd)
    out = None
    for r in range(R):
        part = jnp.einsum('bnlhd,lde->bnhe', chunks + pe[r][:, None, :], w[r])[:, r:r + n_cmp]
        out = part if out is None else out + part
    return out


def selected_attention(q, kb, vb, sel, q_pos):
    take = jax.vmap(jax.vmap(lambda blocks, idx: blocks[idx]))
    kg = take(kb, sel)
    vg = take(vb, sel)
    s = jnp.einsum('bqhgd,bhqnld->bhgqnl', q, kg, preferred_element_type=jnp.float32) * (HEAD_DIM ** -0.5)
    kpos = sel[..., None] * L_SEL + jnp.arange(L_SEL)
    mask = (kpos <= q_pos[:, None, None])[:, :, None]
    B, H, G, Tq, n, L = s.shape
    p = jax.nn.softmax(jnp.where(mask, s, NEG).reshape(B, H, G, Tq, n * L), axis=-1).reshape(B, H, G, Tq, n, L)
    return jnp.einsum('bhgqnl,bhqnld->bqhgd', p.astype(vg.dtype), vg)


def nsa_branches(q, ck, cv, sk, sv, q_pos, w_cmp, pe_cmp):
    B, Tq, H, G, hd = q.shape
    Tk = ck.shape[1]
    kc = compress_rows(ck, w_cmp[0], pe_cmp[0])
    vc = compress_rows(cv, w_cmp[1], pe_cmp[1])
    n_cmp = kc.shape[1]
    c_start = jnp.arange(n_cmp) * CMP_STRIDE
    cmask = (c_start + (L_CMP - 1))[None, :] <= q_pos[:, None]
    s = jnp.einsum('bqhgd,bnhd->bhgqn', q, kc, preferred_element_type=jnp.float32) * (hd ** -0.5)
    p = jnp.where(cmask, jax.nn.softmax(jnp.where(cmask, s, NEG), axis=-1), 0.0)
    o_cmp = jnp.einsum('bhgqn,bnhd->bqhgd', p.astype(vc.dtype), vc)
    n_sel = -(-Tk // L_SEL)
    s_start = jnp.arange(n_sel) * L_SEL
    overlap = jnp.clip(jnp.minimum(c_start[:, None] + L_CMP, s_start[None, :] + L_SEL)
                       - jnp.maximum(c_start[:, None], s_start[None, :]), 0, None).astype(jnp.float32) / L_CMP
    imp = jnp.einsum('bhgqn,nj->bhqj', p, overlap)
    cur = q_pos // L_SEL
    blk = jnp.arange(n_sel)
    forced = (blk[None, :] == 0) | (blk[None, :] == cur[:, None]) | (blk[None, :] == cur[:, None] - 1)
    valid = s_start[None, :] <= q_pos[:, None]
    score = jnp.where(forced, BIG, jnp.where(valid, imp, -BIG))
    n_top = min(N_SEL, n_sel)
    _, sel = lax.top_k(score, n_top)
    pad = n_sel * L_SEL - Tk
    to_blocks = lambda r: jnp.pad(r, ((0, 0), (0, pad), (0, 0), (0, 0))).reshape(B, n_sel, L_SEL, H, hd).transpose(0, 3, 1, 2, 4)
    kb, vb = to_blocks(sk), to_blocks(sv)
    if Tq > SEL_Q_BLOCK and Tq % SEL_Q_BLOCK == 0:
        nb = Tq // SEL_Q_BLOCK
        q_b = q.reshape(B, nb, SEL_Q_BLOCK, H, G, hd).transpose(1, 0, 2, 3, 4, 5)
        sel_b = sel.reshape(B, H, nb, SEL_Q_BLOCK, n_top).transpose(2, 0, 1, 3, 4)
        pos_b = q_pos.reshape(nb, SEL_Q_BLOCK)
        o = lax.map(lambda a: selected_attention(a[0], kb, vb, a[1], a[2]), (q_b, sel_b, pos_b))
        o_slc = o.transpose(1, 0, 2, 3, 4, 5).reshape(B, Tq, H, G, hd)
    else:
        o_slc = selected_attention(q, kb, vb, sel, q_pos)
    return o_cmp, o_slc


def window_attention(q, k, v, q_pos, k_pos):
    s = jnp.einsum('bnqhgd,bnkhd->bnhgqk', q, k, preferred_element_type=jnp.float32) * (HEAD_DIM ** -0.5)
    d = q_pos[:, :, None] - k_pos[:, None, :]
    mask = (d >= 0) & (d < WINDOW) & (k_pos[:, None, :] >= 0)
    p = jax.nn.softmax(jnp.where(mask[:, None, None], s, NEG), axis=-1)
    return jnp.einsum('bnhgqk,bnkhd->bnqhgd', p.astype(v.dtype), v)


def window_prompt(q, k, v):
    B, T, H, G, hd = q.shape
    qb = Q_BLOCK if T % Q_BLOCK == 0 else T
    nb = T // qb
    padw = ((0, 0), (WINDOW, 0), (0, 0), (0, 0))
    idx = jnp.arange(nb)[:, None] * qb + jnp.arange(WINDOW + qb)
    kb = jnp.pad(k, padw)[:, idx]
    vb = jnp.pad(v, padw)[:, idx]
    q_pos = jnp.arange(T).reshape(nb, qb)
    o = window_attention(q.reshape(B, nb, qb, H, G, hd), kb, vb, q_pos, idx - WINDOW)
    return o.reshape(B, T, H, G, hd)


def token_mixer(u, lw, past):
    B, T, _ = u.shape
    cuts = np.cumsum(PROJ_SPLITS)[:-1].tolist()
    (q_sb, k_sb, v_sb, glu_a, glu_b, q_n, ck, cv, sk, sv, wk, wv, gl) = jnp.split(u @ lw['w_in'], cuts, axis=-1)
    hs = lambda t, h: t.reshape(B, T, h, HEAD_DIM)
    q_sb, k_sb, v_sb = hs(q_sb, H_SB), hs(k_sb, H_SB), hs(v_sb, H_SB)
    q_n = q_n.reshape(B, T, H_KV, G_NSA, HEAD_DIM)
    ck, cv, sk, sv, wk, wv = hs(ck, H_KV), hs(cv, H_KV), hs(sk, H_KV), hs(sv, H_KV), hs(wk, H_KV), hs(wv, H_KV)
    sb_rows = jnp.stack([k_sb, v_sb], axis=2)
    nsa_rows = jnp.stack([ck, cv, sk, sv], axis=2)
    win_rows = jnp.stack([wk, wv], axis=2)
    if past is None:
        q_pos = jnp.arange(T)
        ksb, vsb = k_sb, v_sb
        ck_all, cv_all, sk_all, sv_all = ck, cv, sk, sv
        conv_buf = jnp.zeros((B, CONV_W - 1, D_CONV), u.dtype)
        o_win = window_prompt(q_n, wk, wv)
        win_state = win_rows[:, -min(WINDOW, T):]
    else:
        sb_past, nsa_past, win_buf, conv_buf, past_len = past
        q_pos = past_len + jnp.arange(T)
        ksb = jnp.concatenate([sb_past[:, :, 0], k_sb], axis=1)
        vsb = jnp.concatenate([sb_past[:, :, 1], v_sb], axis=1)
        full = jnp.concatenate([nsa_past, nsa_rows], axis=1)
        ck_all, cv_all, sk_all, sv_all = full[:, :, 0], full[:, :, 1], full[:, :, 2], full[:, :, 3]
        win_all = jnp.concatenate([win_buf, win_rows], axis=1)
        n_buf = win_buf.shape[1]
        k_pos_w = past_len - n_buf + jnp.arange(n_buf + T)
        o_win = window_attention(q_n[:, None], win_all[:, None, :, 0], win_all[:, None, :, 1],
                                 q_pos[None], k_pos_w[None])[:, 0]
        win_state = win_all[:, -min(WINDOW, n_buf + T):]
    o_sb = stick_breaking_blocked(q_sb, ksb, vsb, q_pos, jnp.arange(ksb.shape[1]))
    conv_out, conv_state = conformer_conv(glu_a, glu_b, conv_buf, lw['conv_w'], lw['conv_b'], lw['conv_ln_g'], lw['conv_ln_b'])
    o_cmp, o_slc = nsa_branches(q_n, ck_all, cv_all, sk_all, sv_all, q_pos, lw['cmp_w'], lw['cmp_pe'])
    gates = jax.nn.sigmoid(gl + lw['b_gate']).reshape(B, T, 3, H_KV, G_NSA, 1)
    o_nsa = gates[:, :, 0] * o_cmp + gates[:, :, 1] * o_slc + gates[:, :, 2] * o_win
    mixed = jnp.concatenate([head_rms_norm(o_sb, lw['sb_norm_g']), conv_out,
                             head_rms_norm(o_nsa.reshape(B, T, H_NSA, HEAD_DIM), lw['nsa_norm_g'])], axis=-1)
    return mixed @ lw['w_o'], (sb_rows, nsa_rows, win_state, conv_state)


def hier_moe(u, lw):
    B, T, D = u.shape
    N = B * T
    t = u.reshape(N, D)
    g_prob = jax.nn.softmax((t @ lw['w_rg'] + lw['b_rg']).astype(jnp.float32), axis=-1)
    g_p, g_idx = lax.top_k(g_prob, 1)
    e_logits = (t @ lw['w_re'] + lw['b_re']).astype(jnp.float32).reshape(N, N_GROUPS, E_PER_GROUP)
    e_logits = jnp.take_along_axis(e_logits, g_idx[:, :, None], axis=1)[:, 0]
    e_p, e_idx = lax.top_k(jax.nn.softmax(e_logits, axis=-1), TOP_K_IN_GROUP)
    w = g_p * e_p / jnp.sum(e_p, -1, keepdims=True)
    expert = g_idx * E_PER_GROUP + e_idx
    combine = jnp.sum(jax.nn.one_hot(expert, N_EXPERTS, dtype=jnp.float32) * w[..., None], axis=1)
    hg = jnp.einsum('nd,edf->nef', t, lw['w_gate'])
    hu = jnp.einsum('nd,edf->nef', t, lw['w_up'])
    h = jax.nn.silu(hg) * hu * combine[:, :, None].astype(t.dtype)
    return jnp.einsum('nef,efd->nd', h, lw['w_down']).reshape(B, T, D)


def decoder_layer(x, c, lw, past):
    mod = jax.nn.silu(c) @ lw['w_mod'] + lw['b_mod']
    sh1, sc1, g1, sh2, sc2, g2 = jnp.split(mod[:, None, :], 6, axis=-1)
    h, new_state = token_mixer(x * (1 + sc1) + sh1, lw, past)
    x = layer_norm(ALPHA * x + g1 * h, lw['ln1_g'], lw['ln1_b'])
    f = hier_moe(x * (1 + sc2) + sh2, lw)
    x = layer_norm(ALPHA * x + g2 * f, lw['ln2_g'], lw['ln2_b'])
    return x, new_state


def setup_inputs(seed: int = 0) -> dict:
    key = jax.random.key(seed)
    ks = jax.random.split(key, 40)
    f32 = jnp.float32
    nrm = lambda k, shape, s: jax.random.normal(k, shape, f32) * s
    n_pages = PAST_LEN // PAGE_SIZE
    n_used = DEC_BATCH * n_pages
    n_pool = n_used + max(1, n_used // 4)
    w_buf = min(WINDOW, PAST_LEN)
    page_table = jax.random.permutation(ks[0], n_pool)[:n_used].reshape(DEC_BATCH, n_pages).astype(jnp.int32)
    D = D_MODEL
    return {
        'x_prompt': nrm(ks[1], (BATCH, SEQ, D), 1.0),
        'x_sample': nrm(ks[2], (DEC_BATCH, DEC_SEQ, D), 1.0),
        'cache_sb': nrm(ks[3], (DEPTH, n_pool, PAGE_SIZE, 2, H_SB, HEAD_DIM), 1.0),
        'cache_nsa': nrm(ks[4], (DEPTH, n_pool, PAGE_SIZE, 4, H_KV, HEAD_DIM), 1.0),
        'state_win': nrm(ks[5], (DEPTH, DEC_BATCH, w_buf, 2, H_KV, HEAD_DIM), 1.0),
        'state_conv': nrm(ks[6], (DEPTH, DEC_BATCH, CONV_W - 1, D_CONV), 0.5),
        'page_table': page_table,
        'c_prompt': nrm(ks[7], (BATCH, D), 1.0),
        'c_sample': nrm(ks[8], (DEC_BATCH, D), 1.0),
        'w_mod': nrm(ks[9], (DEPTH, D, 6 * D), 0.5 * D ** -0.5),
        'b_mod': nrm(ks[10], (DEPTH, 6 * D), 0.02),
        'w_in': nrm(ks[11], (DEPTH, D, D_IN), D ** -0.5),
        'b_gate': nrm(ks[12], (DEPTH, 3 * H_NSA), 0.01),
        'conv_w': nrm(ks[13], (DEPTH, CONV_W, D_CONV), CONV_W ** -0.5),
        'conv_b': nrm(ks[14], (DEPTH, D_CONV), 0.02),
        'conv_ln_g': 1.0 + nrm(ks[15], (DEPTH, D_CONV), 0.02),
        'conv_ln_b': nrm(ks[16], (DEPTH, D_CONV), 0.02),
        'cmp_w': nrm(ks[17], (DEPTH, 2, L_CMP, HEAD_DIM, HEAD_DIM), (L_CMP * HEAD_DIM) ** -0.5),
        'cmp_pe': nrm(ks[18], (DEPTH, 2, L_CMP, HEAD_DIM), 0.1),
        'sb_norm_g': 1.0 + nrm(ks[19], (DEPTH, D_SB), 0.02),
        'nsa_norm_g': 1.0 + nrm(ks[20], (DEPTH, D_NSA), 0.02),
        'w_o': nrm(ks[21], (DEPTH, D, D), BETA * D ** -0.5),
        'ln1_g': 1.0 + nrm(ks[22], (DEPTH, D), 0.02),
        'ln1_b': nrm(ks[23], (DEPTH, D), 0.02),
        'w_rg': nrm(ks[24], (DEPTH, D, N_GROUPS), D ** -0.5),
        'b_rg': nrm(ks[25], (DEPTH, N_GROUPS), 0.01),
        'w_re': nrm(ks[26], (DEPTH, D, N_EXPERTS), D ** -0.5),
        'b_re': nrm(ks[27], (DEPTH, N_EXPERTS), 0.01),
        'w_gate': nrm(ks[28], (DEPTH, N_EXPERTS, D, D_EXPERT), D ** -0.5),
        'w_up': nrm(ks[29], (DEPTH, N_EXPERTS, D, D_EXPERT), D ** -0.5),
        'w_down': nrm(ks[30], (DEPTH, N_EXPERTS, D_EXPERT, D), BETA * D_EXPERT ** -0.5),
        'ln2_g': 1.0 + nrm(ks[31], (DEPTH, D), 0.02),
        'ln2_b': nrm(ks[32], (DEPTH, D), 0.02),
    }


def reference(x_prompt, x_sample, cache_sb, cache_nsa, state_win, state_conv, page_table, c_prompt, c_sample,
              w_mod, b_mod, w_in, b_gate, conv_w, conv_b, conv_ln_g, conv_ln_b, cmp_w, cmp_pe,
              sb_norm_g, nsa_norm_g, w_o, ln1_g, ln1_b, w_rg, b_rg, w_re, b_re, w_gate, w_up, w_down,
              ln2_g, ln2_b):
    past_len = page_table.shape[1] * cache_sb.shape[2]
    xp, xs = x_prompt, x_sample
    st_p, st_s = [], []
    for l in range(DEPTH):
        lw = dict(w_mod=w_mod[l], b_mod=b_mod[l], w_in=w_in[l], b_gate=b_gate[l], conv_w=conv_w[l],
                  conv_b=conv_b[l], conv_ln_g=conv_ln_g[l], conv_ln_b=conv_ln_b[l], cmp_w=cmp_w[l],
                  cmp_pe=cmp_pe[l], sb_norm_g=sb_norm_g[l], nsa_norm_g=nsa_norm_g[l], w_o=w_o[l],
                  ln1_g=ln1_g[l], ln1_b=ln1_b[l], w_rg=w_rg[l], b_rg=b_rg[l], w_re=w_re[l], b_re=b_re[l],
                  w_gate=w_gate[l], w_up=w_up[l], w_down=w_down[l], ln2_g=ln2_g[l], ln2_b=ln2_b[l])
        xp, sp = decoder_layer(xp, c_prompt, lw, None)
        past = (gather_pages(cache_sb[l], page_table), gather_pages(cache_nsa[l], page_table),
                state_win[l], state_conv[l], past_len)
        xs, ss = decoder_layer(xs, c_sample, lw, past)
        st_p.append(sp)
        st_s.append(ss)
    sb_p, nsa_p, win_p, conv_p = [jnp.stack([s[i] for s in st_p]) for i in range(4)]
    sb_s, nsa_s, win_s, conv_s = [jnp.stack([s[i] for s in st_s]) for i in range(4)]
    return (xp, xs, sb_p, nsa_p, win_p, conv_p, sb_s, nsa_s, win_s, conv_s)
```

```python
import functools

import jax
import jax.numpy as jnp
import numpy as np
from jax import lax
from jax.experimental import pallas as pl
from jax.experimental.pallas import tpu as pltpu

F32 = jnp.float32
BF16 = jnp.bfloat16

HEAD_DIM = 128
H_SB = 8
H_KV = 4
G_NSA = 4
CONV_W = 31
L_CMP = 32
CMP_STRIDE = 16
L_SEL = 64
SEL_SHIFT = 6
N_SEL = 16
WINDOW = 512
N_GROUPS = 4
E_PER_GROUP = 4
D_EXPERT = 512
EPS = 1e-5
NEG = -1e30
BIG = 1e9

LANES = 128
SAMPLE_ROW_TILE = 16
VMEM_LIMIT = 56 * 1024 * 1024


def _cparams(sem):
    return pltpu.CompilerParams(dimension_semantics=sem, vmem_limit_bytes=VMEM_LIMIT)


def _modulate_kernel(x_ref, sc_ref, sh_ref, o_ref):
    o_ref[0] = (x_ref[0] * (1.0 + sc_ref[0]) + sh_ref[0]).astype(o_ref.dtype)


def modulate(x, sc, sh, out_dtype):
    B, T, D = x.shape
    tt = min(T, 512)
    return pl.pallas_call(
        _modulate_kernel,
        grid=(B, T // tt),
        in_specs=[pl.BlockSpec((1, tt, D), lambda b, i: (b, i, 0)),
                  pl.BlockSpec((1, 1, D), lambda b, i: (b, 0, 0)),
                  pl.BlockSpec((1, 1, D), lambda b, i: (b, 0, 0))],
        out_specs=pl.BlockSpec((1, tt, D), lambda b, i: (b, i, 0)),
        out_shape=jax.ShapeDtypeStruct((B, T, D), out_dtype),
        compiler_params=_cparams(("parallel", "parallel")),
        name="modulate",
    )(x, sc, sh)


def _mm_kernel(*refs, n_x, has_bias):
    xs = refs[:n_x]
    ws = refs[n_x:2 * n_x]
    o_ref = refs[-1]
    acc = None
    for x_ref, w_ref in zip(xs, ws):
        part = jnp.dot(x_ref[...].astype(BF16), w_ref[...].astype(BF16), preferred_element_type=F32)
        acc = part if acc is None else acc + part
    if has_bias:
        acc = acc + refs[2 * n_x][...]
    o_ref[...] = acc.astype(o_ref.dtype)


def matmul(xs, w, layer, row_offsets, n_cols, *, tm, tn, bias=None, out_dtype=F32, name="matmul"):
    M = xs[0].shape[0]
    tm = min(tm, M)
    assert M % tm == 0 and n_cols % tn == 0
    in_specs = [pl.BlockSpec((tm, x.shape[1]), lambda i, j: (i, 0)) for x in xs]
    for x, off in zip(xs, row_offsets):
        k = x.shape[1]
        assert off % k == 0
        if layer is None:
            in_specs.append(pl.BlockSpec((k, tn), lambda i, j, o=off // k: (o, j)))
        else:
            in_specs.append(pl.BlockSpec((None, k, tn), lambda i, j, o=off // k: (layer, o, j)))
    args = list(xs) + [w] * len(xs)
    if bias is not None:
        in_specs.append(pl.BlockSpec((1, tn), lambda i, j: (0, j)))
        args.append(bias)
    return pl.pallas_call(
        functools.partial(_mm_kernel, n_x=len(xs), has_bias=bias is not None),
        grid=(M // tm, n_cols // tn),
        in_specs=in_specs,
        out_specs=pl.BlockSpec((tm, tn), lambda i, j: (i, j)),
        out_shape=jax.ShapeDtypeStruct((M, n_cols), out_dtype),
        compiler_params=_cparams(("parallel", "arbitrary")),
        name=name,
    )(*args)


def _log_sigmoid(z):
    return jnp.minimum(z, 0.0) - jnp.log1p(jnp.exp(-jnp.abs(z)))


def _sb_prompt_kernel(q_ref, k_ref, v_ref, g_ref, o_ref, *, blk):
    qi = pl.program_id(2)
    scale = HEAD_DIM ** -0.5
    q = q_ref[...].astype(BF16)
    jj = lax.broadcasted_iota(jnp.int32, (blk, blk), 0)
    ss = lax.broadcasted_iota(jnp.int32, (blk, blk), 1)
    later = jnp.where(jj > ss, 1.0, 0.0).astype(BF16)

    def body(i, carry):
        acc, run = carry
        kb = qi - i
        k0 = pl.multiple_of(kb * blk, blk)
        k = k_ref[pl.ds(k0, blk), :].astype(BF16)
        v = v_ref[pl.ds(k0, blk), :].astype(BF16)
        z = lax.dot_general(q, k, (((1,), (1,)), ((), ())), preferred_element_type=F32) * scale
        mask = (kb * blk + ss) < (qi * blk + jj)
        lb = _log_sigmoid(z)
        l1m = jnp.where(mask, lb - z, 0.0)
        hi = l1m.astype(BF16)
        lo = (l1m - hi.astype(F32)).astype(BF16)
        inblk = (jnp.dot(hi, later, preferred_element_type=F32)
                 + jnp.dot(lo, later, preferred_element_type=F32))
        a = jnp.where(mask, jnp.exp(lb + inblk + run), 0.0)
        acc = acc + jnp.dot(a.astype(BF16), v, preferred_element_type=F32)
        run = run + inblk[:, 0:1] + l1m[:, 0:1]
        return acc, run

    acc, _ = lax.fori_loop(0, qi + 1, body,
                           (jnp.zeros((blk, HEAD_DIM), F32), jnp.zeros((blk, 1), F32)))
    ms = jnp.mean(acc * acc, axis=-1, keepdims=True)
    o_ref[...] = (acc * lax.rsqrt(ms + EPS) * g_ref[...]).astype(o_ref.dtype)


def sb_prompt(proj, gain, B, T, *, blk=256):
    blk = min(blk, T)
    nq = T // blk
    return pl.pallas_call(
        functools.partial(_sb_prompt_kernel, blk=blk),
        grid=(B, H_SB, nq),
        in_specs=[pl.BlockSpec((blk, HEAD_DIM), lambda b, h, i: (b * nq + i, h)),
                  pl.BlockSpec((T, HEAD_DIM), lambda b, h, i: (b, H_SB + h)),
                  pl.BlockSpec((T, HEAD_DIM), lambda b, h, i: (b, 2 * H_SB + h)),
                  pl.BlockSpec((1, HEAD_DIM), lambda b, h, i: (0, h))],
        out_specs=pl.BlockSpec((blk, HEAD_DIM), lambda b, h, i: (b * nq + i, h)),
        out_shape=jax.ShapeDtypeStruct((B * T, H_SB * HEAD_DIM), BF16),
        compiler_params=_cparams(("parallel", "parallel", "arbitrary")),
        name="sb_prompt",
    )(proj, proj, proj, gain)


HIST = 32


def _conv_kernel(a_ref, b_ref, hist_ref, w_ref, cb_ref, g_ref, beta_ref, y_ref, st_ref, ext_ref, *, tb, t_valid):
    i = pl.program_id(1)

    @pl.when(i == 0)
    def _():
        ext_ref[0:HIST, :] = hist_ref[0]

    u = a_ref[...] * jax.nn.sigmoid(b_ref[...])
    ext_ref[HIST:HIST + tb, :] = u
    acc = jnp.zeros(u.shape, F32)
    for k in range(CONV_W):
        acc = acc + ext_ref[pl.ds(HIST - (CONV_W - 1) + k, tb), :] * w_ref[k:k + 1, :]
    y = acc + cb_ref[...]
    mu = jnp.mean(y, axis=-1, keepdims=True)
    var = jnp.mean(jnp.square(y - mu), axis=-1, keepdims=True)
    yn = (y - mu) * lax.rsqrt(var + EPS) * g_ref[...] + beta_ref[...]
    y_ref[...] = (yn * jax.nn.sigmoid(yn)).astype(y_ref.dtype)
    st_ref[0] = ext_ref[pl.ds(HIST + t_valid - (CONV_W - 1), CONV_W - 1), :]
    if tb >= HIST:
        ext_ref[0:HIST, :] = ext_ref[tb:tb + HIST, :]


def conformer_conv(proj, hist, conv_w, conv_b, ln_g, ln_b, B, T, t_valid, *, tb=256):
    C = conv_w.shape[-1]
    tb = min(tb, T)
    nt = T // tb
    assert nt == 1 or (tb >= HIST and t_valid == tb)
    return pl.pallas_call(
        functools.partial(_conv_kernel, tb=tb, t_valid=t_valid),
        grid=(B, nt),
        in_specs=[pl.BlockSpec((tb, C), lambda b, i: (b * nt + i, 3)),
                  pl.BlockSpec((tb, C), lambda b, i: (b * nt + i, 4)),
                  pl.BlockSpec((1, HIST, C), lambda b, i: (b, 0, 0)),
                  pl.BlockSpec((CONV_W, C), lambda b, i: (0, 0)),
                  pl.BlockSpec((1, C), lambda b, i: (0, 0)),
                  pl.BlockSpec((1, C), lambda b, i: (0, 0)),
                  pl.BlockSpec((1, C), lambda b, i: (0, 0))],
        out_specs=[pl.BlockSpec((tb, C), lambda b, i: (b * nt + i, 0)),
                   pl.BlockSpec((1, CONV_W - 1, C), lambda b, i: (b, 0, 0))],
        out_shape=[jax.ShapeDtypeStruct((B * T, C), BF16),
                   jax.ShapeDtypeStruct((B, CONV_W - 1, C), F32)],
        scratch_shapes=[pltpu.VMEM((HIST + tb, C), F32)],
        compiler_params=_cparams(("parallel", "arbitrary")),
        name="conformer_conv",
    )(proj, proj, hist, conv_w, conv_b, ln_g, ln_b)


def _compress_kernel(x_ref, w_ref, pe_ref, o_ref, *, n_chunks):
    parts = []
    for r in range(L_CMP // CMP_STRIDE):
        p = jnp.zeros((n_chunks, HEAD_DIM), F32)
        for l in range(CMP_STRIDE):
            j = r * CMP_STRIDE + l
            rows = x_ref[pl.ds(l, n_chunks, stride=CMP_STRIDE), :] + pe_ref[j:j + 1, :]
            p = p + jnp.dot(rows.astype(BF16), w_ref[j].astype(BF16), preferred_element_type=F32)
        parts.append(p)
    o_ref[...] = parts[0] + pltpu.roll(parts[1], n_chunks - 1, 0)


def compress_kv(rows, col_block0, cmp_w, cmp_pe, layer, B, T):
    n_chunks = T // CMP_STRIDE
    return pl.pallas_call(
        functools.partial(_compress_kernel, n_chunks=n_chunks),
        grid=(B, 2, H_KV),
        in_specs=[pl.BlockSpec((T, HEAD_DIM), lambda b, ty, h: (b, col_block0 + ty * H_KV + h)),
                  pl.BlockSpec((None, None, L_CMP, HEAD_DIM, HEAD_DIM), lambda b, ty, h: (layer, ty, 0, 0, 0)),
                  pl.BlockSpec((None, None, L_CMP, HEAD_DIM), lambda b, ty, h: (layer, ty, 0, 0))],
        out_specs=pl.BlockSpec((None, None, None, n_chunks, HEAD_DIM), lambda b, ty, h: (b, ty, h, 0, 0)),
        out_shape=jax.ShapeDtypeStruct((B, 2, H_KV, n_chunks, HEAD_DIM), F32),
        compiler_params=_cparams(("parallel", "parallel", "parallel")),
        name="nsa_compress",
    )(rows, cmp_w, cmp_pe)


def _nsa_prompt_kernel(q_ref, kc_ref, vc_ref, sk_ref, sv_ref, wk_ref, wv_ref, gl_ref, bg_ref, gain_ref,
                       o_ref, *, tq, n_sel):
    qi = pl.program_id(2)
    scale = HEAD_DIM ** -0.5
    rows = G_NSA * tq
    q0 = qi * tq
    q = jnp.concatenate([q_ref[:, g * HEAD_DIM:(g + 1) * HEAD_DIM] for g in range(G_NSA)], axis=0).astype(BF16)
    lane = lax.broadcasted_iota(jnp.int32, (rows, LANES), 1)
    qpos = q0 + (lax.broadcasted_iota(jnp.int32, (rows, LANES), 0) & (tq - 1))
    nt = (((1,), (1,)), ((), ()))

    n_cmp_pad = kc_ref.shape[0]
    s = lax.dot_general(q, kc_ref[...].astype(BF16), nt, preferred_element_type=F32) * scale
    cl = lax.broadcasted_iota(jnp.int32, (rows, n_cmp_pad), 1)
    qp = q0 + (lax.broadcasted_iota(jnp.int32, (rows, n_cmp_pad), 0) & (tq - 1))
    cmask = (cl * CMP_STRIDE + (L_CMP - 1)) <= qp
    sm = jnp.where(cmask, s, NEG)
    e = jnp.where(cmask, jnp.exp(sm - jnp.max(sm, axis=-1, keepdims=True)), 0.0)
    den = jnp.sum(e, axis=-1, keepdims=True)
    p = e / jnp.where(den > 0.0, den, 1.0)
    pb = p.astype(BF16)
    o_cmp = jnp.dot(pb, vc_ref[...].astype(BF16), preferred_element_type=F32)

    cn = lax.broadcasted_iota(jnp.int32, (n_cmp_pad, LANES), 0) * CMP_STRIDE
    sj = lax.broadcasted_iota(jnp.int32, (n_cmp_pad, LANES), 1) * L_SEL
    ov = jnp.clip(jnp.minimum(cn + L_CMP, sj + L_SEL) - jnp.maximum(cn, sj), 0, None).astype(F32) / L_CMP
    ov = ov.astype(BF16)
    imp = None
    for g in range(G_NSA):
        part = jnp.dot(pb[g * tq:(g + 1) * tq], ov, preferred_element_type=F32)
        imp = part if imp is None else imp + part
    blk = lax.broadcasted_iota(jnp.int32, (tq, LANES), 1)
    qpt = q0 + lax.broadcasted_iota(jnp.int32, (tq, LANES), 0)
    cur = jnp.right_shift(qpt, SEL_SHIFT)
    forced = (blk == 0) | (blk == cur) | (blk == cur - 1)
    valid = blk * L_SEL <= qpt
    score = jnp.where(forced, BIG, jnp.where(valid, imp, -BIG))
    score = jnp.where(blk < n_sel, score, -3e38)
    rank = jnp.zeros((tq, LANES), jnp.int32)
    for i in range(n_sel):
        ci = score[:, i:i + 1]
        ahead = (ci > score) | ((ci == score) & (blk > i))
        rank = rank + ahead.astype(jnp.int32)
    sel = jnp.where((rank < min(N_SEL, n_sel)) & (blk < n_sel), 1.0, 0.0).astype(BF16)

    kj = lax.broadcasted_iota(jnp.int32, (LANES, LANES), 0)
    ks = lax.broadcasted_iota(jnp.int32, (LANES, LANES), 1)

    def flash_step(k_ref, v_ref, kb, allowed_fn, carry):
        m, l, acc = carry
        k0 = pl.multiple_of(kb * LANES, LANES)
        k = k_ref[pl.ds(k0, LANES), :].astype(BF16)
        v = v_ref[pl.ds(k0, LANES), :].astype(BF16)
        sc = lax.dot_general(q, k, nt, preferred_element_type=F32) * scale
        allowed = allowed_fn(kb * LANES + lane)
        scm = jnp.where(allowed, sc, NEG)
        m_new = jnp.maximum(m, jnp.max(scm, axis=-1, keepdims=True))
        alpha = jnp.exp(m - m_new)
        pe = jnp.where(allowed, jnp.exp(scm - m_new), 0.0)
        l = alpha * l + jnp.sum(pe, axis=-1, keepdims=True)
        acc = alpha * acc + jnp.dot(pe.astype(BF16), v, preferred_element_type=F32)
        return m_new, l, acc

    init = (jnp.full((rows, 1), NEG, F32), jnp.zeros((rows, 1), F32), jnp.zeros((rows, HEAD_DIM), F32))

    def slc_body(kb, carry):
        expand = jnp.where(jnp.right_shift(kb * LANES + ks, SEL_SHIFT) == kj, 1.0, 0.0).astype(BF16)
        selk = jnp.dot(sel, expand, preferred_element_type=F32)
        selk = jnp.concatenate([selk] * G_NSA, axis=0)
        return flash_step(sk_ref, sv_ref, kb, lambda kpos: (selk > 0.5) & (kpos <= qpos), carry)

    _, l_s, acc_s = lax.fori_loop(0, qi + 1, slc_body, init)
    o_slc = acc_s / l_s

    def win_body(kb, carry):
        return flash_step(wk_ref, wv_ref, kb,
                          lambda kpos: (qpos - kpos >= 0) & (qpos - kpos < WINDOW), carry)

    first = jnp.maximum(qi - (WINDOW // tq), 0)
    _, l_w, acc_w = lax.fori_loop(first, qi + 1, win_body, init)
    o_win = acc_w / l_w

    gates = jax.nn.sigmoid(gl_ref[...] + bg_ref[...])
    for g in range(G_NSA):
        r = slice(g * tq, (g + 1) * tq)
        o = (gates[:, g:g + 1] * o_cmp[r] + gates[:, G_NSA + g:G_NSA + g + 1] * o_slc[r]
             + gates[:, 2 * G_NSA + g:2 * G_NSA + g + 1] * o_win[r])
        ms = jnp.mean(o * o, axis=-1, keepdims=True)
        o = o * lax.rsqrt(ms + EPS) * gain_ref[:, g * HEAD_DIM:(g + 1) * HEAD_DIM]
        o_ref[:, g * HEAD_DIM:(g + 1) * HEAD_DIM] = o.astype(o_ref.dtype)


def nsa_prompt(proj, kvc, gate_logits, b_gate, gain, B, T):
    tq = LANES
    assert T % tq == 0 and WINDOW % tq == 0
    nq = T // tq
    n_sel = -(-T // L_SEL)
    assert n_sel <= LANES
    n_cmp_pad = kvc.shape[3]
    qw = G_NSA * HEAD_DIM
    seq = lambda c0: pl.BlockSpec((T, HEAD_DIM), lambda b, h, i: (b, c0 + h))
    return pl.pallas_call(
        functools.partial(_nsa_prompt_kernel, tq=tq, n_sel=n_sel),
        grid=(B, H_KV, nq),
        in_specs=[pl.BlockSpec((tq, qw), lambda b, h, i: (b * nq + i, 5120 // qw + h)),
                  pl.BlockSpec((None, None, None, n_cmp_pad, HEAD_DIM), lambda b, h, i: (b, 0, h, 0, 0)),
                  pl.BlockSpec((None, None, None, n_cmp_pad, HEAD_DIM), lambda b, h, i: (b, 1, h, 0, 0)),
                  seq(64), seq(68), seq(72), seq(76),
                  pl.BlockSpec((tq, LANES), lambda b, h, i: (b * nq + i, h)),
                  pl.BlockSpec((1, LANES), lambda b, h, i: (0, h)),
                  pl.BlockSpec((1, qw), lambda b, h, i: (0, h))],
        out_specs=pl.BlockSpec((tq, qw), lambda b, h, i: (b * nq + i, h)),
        out_shape=jax.ShapeDtypeStruct((B * T, H_KV * qw), BF16),
        compiler_params=_cparams(("parallel", "parallel", "arbitrary")),
        name="nsa_prompt",
    )(proj, kvc, kvc, proj, proj, proj, proj, gate_logits, b_gate, gain)


def _layer_norm(y, g, b):
    mu = jnp.mean(y, axis=-1, keepdims=True)
    var = jnp.mean(jnp.square(y - mu), axis=-1, keepdims=True)
    return (y - mu) * lax.rsqrt(var + EPS) * g + b


def _ln1_kernel(x_ref, h_ref, g1_ref, lg_ref, lb_ref, sc_ref, sh_ref, wr_ref, br_ref, x1_ref, x2_ref, rl_ref, *, alpha):
    x1 = _layer_norm(alpha * x_ref[0] + g1_ref[0] * h_ref[0], lg_ref[...], lb_ref[...])
    x1_ref[0] = x1
    x2 = x1 * (1.0 + sc_ref[0]) + sh_ref[0]
    x2_ref[0] = x2.astype(x2_ref.dtype)
    rl_ref[0] = jnp.dot(x2.astype(BF16), wr_ref[...].astype(BF16), preferred_element_type=F32) + br_ref[...]


def post_mixer(x, h, g1, ln_g, ln_b, sc2, sh2, w_router, b_router, alpha):
    B, T, D = x.shape
    tt = min(T, 256)
    row = pl.BlockSpec((1, tt, D), lambda b, i: (b, i, 0))
    per_b = pl.BlockSpec((1, 1, D), lambda b, i: (b, 0, 0))
    vec = pl.BlockSpec((1, D), lambda b, i: (0, 0))
    return pl.pallas_call(
        functools.partial(_ln1_kernel, alpha=alpha),
        grid=(B, T // tt),
        in_specs=[row, row, per_b, vec, vec, per_b, per_b,
                  pl.BlockSpec((D, LANES), lambda b, i: (0, 0)),
                  pl.BlockSpec((1, LANES), lambda b, i: (0, 0))],
        out_specs=[row, row, pl.BlockSpec((1, tt, LANES), lambda b, i: (b, i, 0))],
        out_shape=[jax.ShapeDtypeStruct((B, T, D), F32), jax.ShapeDtypeStruct((B, T, D), BF16),
                   jax.ShapeDtypeStruct((B, T, LANES), F32)],
        compiler_params=_cparams(("parallel", "parallel")),
        name="post_mixer_ln",
    )(x, h, g1, ln_g, ln_b, sc2, sh2, w_router, b_router)


def _ln2_kernel(x_ref, f_ref, g2_ref, lg_ref, lb_ref, o_ref, *, alpha):
    o_ref[0] = _layer_norm(alpha * x_ref[0] + g2_ref[0] * f_ref[0], lg_ref[...], lb_ref[...])


def post_ffn(x, f, g2, ln_g, ln_b, alpha):
    B, T, D = x.shape
    tt = min(T, 256)
    row = pl.BlockSpec((1, tt, D), lambda b, i: (b, i, 0))
    return pl.pallas_call(
        functools.partial(_ln2_kernel, alpha=alpha),
        grid=(B, T // tt),
        in_specs=[row, row, pl.BlockSpec((1, 1, D), lambda b, i: (b, 0, 0)),
                  pl.BlockSpec((1, D), lambda b, i: (0, 0)), pl.BlockSpec((1, D), lambda b, i: (0, 0))],
        out_specs=row,
        out_shape=jax.ShapeDtypeStruct((B, T, D), F32),
        compiler_params=_cparams(("parallel", "parallel")),
        name="post_ffn_ln",
    )(x, f, g2, ln_g, ln_b)


GATHER_WINDOW = 16


def _gather_kernel(idx_ref, src_ref, dst_ref, sem, *, n_rows):
    def copy(i):
        return pltpu.make_async_copy(src_ref.at[pl.ds(idx_ref[i], 1), :], dst_ref.at[pl.ds(i, 1), :], sem)

    window = min(GATHER_WINDOW, n_rows)
    for i in range(window):
        copy(i).start()

    def body(i, c):
        copy(i - window).wait()
        copy(i).start()
        return c

    lax.fori_loop(window, n_rows, body, 0)
    for i in range(window):
        copy(n_rows - window + i).wait()


def gather_rows(src, idx):
    n_rows = idx.shape[0]
    return pl.pallas_call(
        functools.partial(_gather_kernel, n_rows=n_rows),
        grid_spec=pltpu.PrefetchScalarGridSpec(
            num_scalar_prefetch=1, grid=(1,),
            in_specs=[pl.BlockSpec(memory_space=pl.ANY)],
            out_specs=pl.BlockSpec(memory_space=pl.ANY),
            scratch_shapes=[pltpu.SemaphoreType.DMA(())]),
        out_shape=jax.ShapeDtypeStruct((n_rows, src.shape[1]), src.dtype),
        compiler_params=pltpu.CompilerParams(dimension_semantics=("arbitrary",)),
        name="gather_rows",
    )(idx, src)


def _moe_up_kernel(tg_ref, nu_ref, x_ref, wg_ref, wu_ref, cw_ref, h_ref):
    t = pl.program_id(1)

    @pl.when(t < nu_ref[0])
    def _():
        x = x_ref[...].astype(BF16)
        hg = jnp.dot(x, wg_ref[...].astype(BF16), preferred_element_type=F32)
        hu = jnp.dot(x, wu_ref[...].astype(BF16), preferred_element_type=F32)
        h_ref[...] = (hg * jax.nn.sigmoid(hg) * hu * cw_ref[...]).astype(h_ref.dtype)

    @pl.when(t >= nu_ref[0])
    def _():
        h_ref[...] = jnp.zeros(h_ref.shape, h_ref.dtype)


def _moe_down_kernel(tg_ref, nu_ref, h_ref, wd_ref, y_ref):
    t = pl.program_id(1)

    @pl.when(t < nu_ref[0])
    def _():
        y_ref[...] = jnp.dot(h_ref[...], wd_ref[...].astype(BF16), preferred_element_type=F32)

    @pl.when(t >= nu_ref[0])
    def _():
        y_ref[...] = jnp.zeros(y_ref.shape, y_ref.dtype)


def moe_experts(xs, cw, tile_group, n_used, w_gate, w_up, w_down, layer, *, tm):
    n_pad, D = xs.shape
    n_tiles = n_pad // tm
    F = D_EXPERT
    fc = F // 2
    per = F // fc
    w_spec = pl.BlockSpec((None, None, D, fc),
                          lambda c, t, tg, nu: (layer, tg[t] * E_PER_GROUP + c // per, 0, c % per))
    h = pl.pallas_call(
        _moe_up_kernel,
        grid_spec=pltpu.PrefetchScalarGridSpec(
            num_scalar_prefetch=2, grid=(E_PER_GROUP * per, n_tiles),
            in_specs=[pl.BlockSpec((tm, D), lambda c, t, tg, nu: (t, 0)), w_spec, w_spec,
                      pl.BlockSpec((None, tm, 1), lambda c, t, tg, nu: (c // per, t, 0))],
            out_specs=pl.BlockSpec((tm, fc), lambda c, t, tg, nu: (t, c))),
        out_shape=jax.ShapeDtypeStruct((n_pad, E_PER_GROUP * F), BF16),
        compiler_params=_cparams(("arbitrary", "arbitrary")),
        name="moe_gate_up",
    )(tile_group, n_used, xs, w_gate, w_up, cw)
    depth = w_down.shape[0]
    wd = w_down.reshape(depth, N_GROUPS, E_PER_GROUP * F, D)
    tn = 1024
    return pl.pallas_call(
        _moe_down_kernel,
        grid_spec=pltpu.PrefetchScalarGridSpec(
            num_scalar_prefetch=2, grid=(D // tn, n_tiles),
            in_specs=[pl.BlockSpec((tm, E_PER_GROUP * F), lambda c, t, tg, nu: (t, 0)),
                      pl.BlockSpec((None, None, E_PER_GROUP * F, tn), lambda c, t, tg, nu: (layer, tg[t], 0, c))],
            out_specs=pl.BlockSpec((tm, tn), lambda c, t, tg, nu: (t, c))),
        out_shape=jax.ShapeDtypeStruct((n_pad, D), F32),
        compiler_params=_cparams(("arbitrary", "arbitrary")),
        name="moe_down",
    )(tile_group, n_used, h, wd)


def hier_moe(x2, rl, w_gate, w_up, w_down, layer, *, tm):
    N, D = x2.shape
    g_prob = jax.nn.softmax(rl[:, :N_GROUPS], axis=-1)
    g_p, g_idx = lax.top_k(g_prob, 1)
    e_logits = rl[:, N_GROUPS:N_GROUPS + N_GROUPS * E_PER_GROUP].reshape(N, N_GROUPS, E_PER_GROUP)
    e_logits = jnp.take_along_axis(e_logits, g_idx[:, :, None], axis=1)[:, 0]
    e_p, e_idx = lax.top_k(jax.nn.softmax(e_logits, axis=-1), 2)
    w = g_p * e_p / jnp.sum(e_p, -1, keepdims=True)
    cw = jnp.sum(jax.nn.one_hot(e_idx, E_PER_GROUP, dtype=F32) * w[..., None], axis=1)
    grp = g_idx[:, 0].astype(jnp.int32)
    tm = min(tm, N)
    n_tiles = N // tm + N_GROUPS
    n_pad = n_tiles * tm
    order = jnp.argsort(grp, stable=True).astype(jnp.int32)
    grp_sorted = grp[order]
    counts = jnp.sum(jax.nn.one_hot(grp, N_GROUPS, dtype=jnp.int32), axis=0)
    padded = ((counts + tm - 1) // tm) * tm
    start = jnp.cumsum(counts) - counts
    pstart = jnp.cumsum(padded) - padded
    dest = pstart[grp_sorted] + jnp.arange(N, dtype=jnp.int32) - start[grp_sorted]
    src_idx = jnp.zeros((n_pad,), jnp.int32).at[dest].set(order)
    cw_sorted = jnp.zeros((n_pad, E_PER_GROUP), F32).at[dest].set(cw[order])
    pos = jnp.zeros((N,), jnp.int32).at[order].set(dest)
    pend = jnp.cumsum(padded)
    tile_start = jnp.arange(n_tiles, dtype=jnp.int32) * tm
    tile_group = jnp.minimum(jnp.sum((tile_start[:, None] >= pend[None, :]).astype(jnp.int32), axis=1),
                             N_GROUPS - 1).astype(jnp.int32)
    n_used = (pend[-1] // tm).astype(jnp.int32).reshape(1)
    x_words = lax.bitcast_convert_type(x2.reshape(N, D // 2, 2), jnp.uint32)
    xs = lax.bitcast_convert_type(gather_rows(x_words, src_idx), BF16).reshape(n_pad, D)
    ys = moe_experts(xs, cw_sorted.T[:, :, None], tile_group, n_used, w_gate, w_up, w_down, layer, tm=tm)
    return gather_rows(ys, pos)


def _stick_breaking_ref(q, k, v, q_pos, k_pos):
    z = jnp.einsum('bqhd,bkhd->bhqk', q, k, preferred_element_type=F32) * (HEAD_DIM ** -0.5)
    mask = k_pos[None, :] < q_pos[:, None]
    log_beta = jax.nn.log_sigmoid(z)
    log_1m = jnp.where(mask, log_beta - z, 0.0)
    after = lax.cumsum(log_1m, axis=3, reverse=True) - log_1m
    a = jnp.where(mask, jnp.exp(log_beta + after), 0.0)
    return jnp.einsum('bhqk,bkhd->bqhd', a.astype(v.dtype), v)


def _gather_pages(pool, page_table):
    g = pool[page_table]
    return g.reshape((g.shape[0], g.shape[1] * g.shape[2]) + g.shape[3:])


def _head_rms(o, g):
    B, T, H, hd = o.shape
    of = o * lax.rsqrt(jnp.mean(o * o, -1, keepdims=True) + EPS)
    return of.reshape(B, T, H * hd) * g


def _sample_attention(proj, gate_logits, b_gate, sb_past, nsa_past, win_buf, kvc, sb_g, nsa_g, past_len):
    B, T, _ = proj.shape
    hs = lambda c0, h: proj[:, :, c0:c0 + h * HEAD_DIM].reshape(B, T, h, HEAD_DIM)
    q_sb, k_sb, v_sb = hs(0, H_SB), hs(1024, H_SB), hs(2048, H_SB)
    q_n = proj[:, :, 5120:7168].reshape(B, T, H_KV, G_NSA, HEAD_DIM)
    sk, sv, wk, wv = hs(8192, H_KV), hs(8704, H_KV), hs(9216, H_KV), hs(9728, H_KV)
    q_pos = past_len + jnp.arange(T)
    ksb = jnp.concatenate([sb_past[:, :, 0], k_sb], axis=1)
    vsb = jnp.concatenate([sb_past[:, :, 1], v_sb], axis=1)
    o_sb = _stick_breaking_ref(q_sb, ksb, vsb, q_pos, jnp.arange(ksb.shape[1]))
    kc, vc = kvc[:, 0], kvc[:, 1]
    n_pad = kc.shape[2]
    c_start = jnp.arange(n_pad) * CMP_STRIDE
    cmask = (c_start + (L_CMP - 1))[None, :] <= q_pos[:, None]
    s = jnp.einsum('bqhgd,bhnd->bhgqn', q_n, kc, preferred_element_type=F32) * (HEAD_DIM ** -0.5)
    p = jnp.where(cmask, jax.nn.softmax(jnp.where(cmask, s, NEG), axis=-1), 0.0)
    o_cmp = jnp.einsum('bhgqn,bhnd->bqhgd', p, vc)
    Tk = past_len + T
    n_sel = -(-Tk // L_SEL)
    s_start = jnp.arange(n_sel) * L_SEL
    overlap = jnp.clip(jnp.minimum(c_start[:, None] + L_CMP, s_start[None, :] + L_SEL)
                       - jnp.maximum(c_start[:, None], s_start[None, :]), 0, None).astype(F32) / L_CMP
    imp = jnp.einsum('bhgqn,nj->bhqj', p, overlap)
    cur = q_pos // L_SEL
    blk = jnp.arange(n_sel)
    forced = (blk[None, :] == 0) | (blk[None, :] == cur[:, None]) | (blk[None, :] == cur[:, None] - 1)
    valid = s_start[None, :] <= q_pos[:, None]
    score = jnp.where(forced, BIG, jnp.where(valid, imp, -BIG))
    n_top = min(N_SEL, n_sel)
    _, sel = lax.top_k(score, n_top)
    sk_all = jnp.concatenate([nsa_past[:, :, 2], sk], axis=1)
    sv_all = jnp.concatenate([nsa_past[:, :, 3], sv], axis=1)
    pad = n_sel * L_SEL - Tk
    to_blocks = lambda r: jnp.pad(r, ((0, 0), (0, pad), (0, 0), (0, 0))).reshape(
        B, n_sel, L_SEL, H_KV, HEAD_DIM).transpose(0, 3, 1, 2, 4)
    kb, vb = to_blocks(sk_all), to_blocks(sv_all)
    take = jax.vmap(jax.vmap(lambda blocks, idx: blocks[idx]))
    kg, vg = take(kb, sel), take(vb, sel)
    s2 = jnp.einsum('bqhgd,bhqnld->bhgqnl', q_n, kg, preferred_element_type=F32) * (HEAD_DIM ** -0.5)
    kpos = sel[..., None] * L_SEL + jnp.arange(L_SEL)
    m2 = (kpos <= q_pos[:, None, None])[:, :, None]
    shp = s2.shape
    p2 = jax.nn.softmax(jnp.where(m2, s2, NEG).reshape(shp[:4] + (-1,)), axis=-1).reshape(shp)
    o_slc = jnp.einsum('bhgqnl,bhqnld->bqhgd', p2, vg)
    n_buf = win_buf.shape[1]
    wk_all = jnp.concatenate([win_buf[:, :, 0], wk], axis=1)
    wv_all = jnp.concatenate([win_buf[:, :, 1], wv], axis=1)
    k_pos_w = past_len - n_buf + jnp.arange(n_buf + T)
    s3 = jnp.einsum('bqhgd,bkhd->bhgqk', q_n, wk_all, preferred_element_type=F32) * (HEAD_DIM ** -0.5)
    d = q_pos[:, None] - k_pos_w[None, :]
    m3 = (d >= 0) & (d < WINDOW) & (k_pos_w[None, :] >= 0)
    p3 = jax.nn.softmax(jnp.where(m3, s3, NEG), axis=-1)
    o_win = jnp.einsum('bhgqk,bkhd->bqhgd', p3, wv_all)
    gates = jax.nn.sigmoid(gate_logits + b_gate).reshape(B, T, 3, H_KV, G_NSA, 1)
    o_nsa = gates[:, :, 0] * o_cmp + gates[:, :, 1] * o_slc + gates[:, :, 2] * o_win
    win_state = jnp.concatenate([win_buf, jnp.stack([wk, wv], axis=2)], axis=1)[:, -min(WINDOW, n_buf + T):]
    return (_head_rms(o_sb, sb_g), _head_rms(o_nsa.reshape(B, T, H_KV * G_NSA, HEAD_DIM), nsa_g), win_state)


def _gate_weight(w_in, layer, n_main):
    D = w_in.shape[1]
    wg = w_in[layer, :, n_main:n_main + 3 * H_KV * G_NSA].reshape(D, 3, H_KV, G_NSA)
    wg = wg.transpose(0, 2, 1, 3).reshape(D, H_KV, 3 * G_NSA)
    return jnp.pad(wg, ((0, 0), (0, 0), (0, LANES - 3 * G_NSA))).reshape(D, H_KV * LANES)


def _regroup_gate_bias(b):
    bg = b.reshape(3, H_KV, G_NSA).transpose(1, 0, 2).reshape(H_KV, 3 * G_NSA)
    return jnp.pad(bg, ((0, 0), (0, LANES - 3 * G_NSA))).reshape(1, H_KV * LANES)


def _layer(l, x, mod, past, W, alpha, t_real):
    B, T, D = x.shape
    N = B * T
    n_main = 10240
    sh1, sc1, g1, sh2, sc2, g2 = [m[:, None, :] for m in jnp.split(mod, 6, axis=-1)]
    u = modulate(x, sc1, sh1, BF16).reshape(N, D)
    tm = 1024
    proj = matmul([u], W['w_in'], l, [0], n_main, tm=tm, tn=512, name="in_proj")
    gate_logits = matmul([u], _gate_weight(W['w_in'], l, n_main), None, [0], H_KV * LANES, tm=tm, tn=512,
                         name="gate_proj")
    bg = _regroup_gate_bias(W['b_gate'][l])
    proj3 = proj.reshape(B, T, n_main)[:, :t_real]
    sb_rows = proj3[:, :, 1024:3072].reshape(B, t_real, 2, H_SB, HEAD_DIM)
    nsa_rows = proj3[:, :, 7168:9216].reshape(B, t_real, 4, H_KV, HEAD_DIM)
    win_rows = proj3[:, :, 9216:10240].reshape(B, t_real, 2, H_KV, HEAD_DIM)
    sb_g = W['sb_norm_g'][l][None, :]
    nsa_g = W['nsa_norm_g'][l][None, :]
    conv_args = (W['conv_w'][l], W['conv_b'][l][None, :], W['conv_ln_g'][l][None, :], W['conv_ln_b'][l][None, :])
    if past is None:
        x_sb = sb_prompt(proj, sb_g, B, T)
        hist = jnp.zeros((B, HIST, D // 4), F32)
        x_conv, conv_state = conformer_conv(proj, hist, *conv_args, B, T, min(T, 256))
        kvc = compress_kv(proj, 56, W['cmp_w'], W['cmp_pe'], l, B, T)
        x_nsa = nsa_prompt(proj, kvc, gate_logits, bg, nsa_g, B, T)
        win_state = win_rows[:, -min(WINDOW, T):]
    else:
        sb_past, nsa_past, win_buf, conv_buf, past_len = past
        hist = jnp.pad(conv_buf, ((0, 0), (HIST - (CONV_W - 1), 0), (0, 0)))
        x_conv, conv_state = conformer_conv(proj, hist, *conv_args, B, T, t_real)
        assert past_len % CMP_STRIDE == 0 and t_real < CMP_STRIDE
        kvc = compress_kv(nsa_past.reshape(B * past_len, -1), 0, W['cmp_w'], W['cmp_pe'], l, B, past_len)
        glog = gate_logits.reshape(B, T, H_KV, LANES)[:, :t_real, :, :3 * G_NSA].reshape(B, t_real, H_KV, 3, G_NSA)
        glog = glog.transpose(0, 1, 3, 2, 4).reshape(B, t_real, 3 * H_KV * G_NSA)
        o_sb, o_nsa, win_state = _sample_attention(proj3, glog, W['b_gate'][l], sb_past, nsa_past, win_buf, kvc,
                                                   W['sb_norm_g'][l], W['nsa_norm_g'][l], past_len)
        pad_rows = lambda o: jnp.pad(o, ((0, 0), (0, T - t_real), (0, 0))).reshape(N, -1).astype(BF16)
        x_sb, x_nsa = pad_rows(o_sb), pad_rows(o_nsa)
    h = matmul([x_sb, x_conv, x_nsa], W['w_o'], l, [0, 1024, 2048], D, tm=tm, tn=512, name="out_proj")
    w_router = jnp.pad(jnp.concatenate([W['w_rg'][l], W['w_re'][l]], axis=1), ((0, 0), (0, LANES - 20)))
    b_router = jnp.pad(jnp.concatenate([W['b_rg'][l], W['b_re'][l]]), (0, LANES - 20))[None, :]
    x1, x2, rl = post_mixer(x, h.reshape(B, T, D), g1, W['ln1_g'][l][None, :], W['ln1_b'][l][None, :],
                            sc2, sh2, w_router, b_router, alpha)
    f = hier_moe(x2.reshape(N, D), rl.reshape(N, LANES), W['w_gate'], W['w_up'], W['w_down'], l, tm=512)
    x_out = post_ffn(x1, f.reshape(B, T, D), g2, W['ln2_g'][l][None, :], W['ln2_b'][l][None, :], alpha)
    return x_out, (sb_rows, nsa_rows, win_state, conv_state)


def kernel(x_prompt, x_sample, cache_sb, cache_nsa, state_win, state_conv, page_table, c_prompt, c_sample, w_mod, b_mod, w_in, b_gate, conv_w, conv_b, conv_ln_g, conv_ln_b, cmp_w, cmp_pe, sb_norm_g, nsa_norm_g, w_o, ln1_g, ln1_b, w_rg, b_rg, w_re, b_re, w_gate, w_up, w_down, ln2_g, ln2_b):
    depth = w_mod.shape[0]
    alpha = (2.0 * depth) ** 0.25
    W = dict(w_in=w_in, b_gate=b_gate, conv_w=conv_w, conv_b=conv_b, conv_ln_g=conv_ln_g, conv_ln_b=conv_ln_b,
             cmp_w=cmp_w, cmp_pe=cmp_pe, sb_norm_g=sb_norm_g, nsa_norm_g=nsa_norm_g, w_o=w_o, ln1_g=ln1_g,
             ln1_b=ln1_b, w_rg=w_rg, b_rg=b_rg, w_re=w_re, b_re=b_re, w_gate=w_gate, w_up=w_up, w_down=w_down,
             ln2_g=ln2_g, ln2_b=ln2_b)
    past_len = page_table.shape[1] * cache_sb.shape[2]
    Bp, Bs = c_prompt.shape[0], c_sample.shape[0]
    c_all = jnp.concatenate([c_prompt, c_sample], axis=0)
    c_rows = -(-(Bp + Bs) // 16) * 16
    c_act = jnp.pad(jax.nn.silu(c_all), ((0, c_rows - Bp - Bs), (0, 0)))
    t_s = x_sample.shape[1]
    t_pad = -(-t_s // SAMPLE_ROW_TILE) * SAMPLE_ROW_TILE
    xp = x_prompt
    xs = jnp.pad(x_sample, ((0, 0), (0, t_pad - t_s), (0, 0)))
    st_p, st_s = [], []
    for l in range(depth):
        mod = matmul([c_act], w_mod, l, [0], w_mod.shape[2], tm=c_rows, tn=512, bias=b_mod[l][None, :],
                     name="adaln_mod")
        xp, sp = _layer(l, xp, mod[:Bp], None, W, alpha, xp.shape[1])
        past = (_gather_pages(cache_sb[l], page_table), _gather_pages(cache_nsa[l], page_table),
                state_win[l], state_conv[l], past_len)
        xs, ss = _layer(l, xs, mod[Bp:Bp + Bs], past, W, alpha, t_s)
        st_p.append(sp)
        st_s.append(ss)
    sb_p, nsa_p, win_p, conv_p = [jnp.stack([s[i] for s in st_p]) for i in range(4)]
    sb_s, nsa_s, win_s, conv_s = [jnp.stack([s[i] for s in st_s]) for i in range(4)]
    return (xp, xs[:, :t_s], sb_p, nsa_p, win_p, conv_p, sb_s, nsa_s, win_s, conv_s)
```

```python
import functools

import jax
import jax.numpy as jnp
import numpy as np
from jax import lax
from jax.experimental import pallas as pl
from jax.experimental.pallas import tpu as pltpu

F32 = jnp.float32
BF16 = jnp.bfloat16

HEAD_DIM = 128
H_SB = 8
H_KV = 4
G_NSA = 4
CONV_W = 31
L_CMP = 32
CMP_STRIDE = 16
L_SEL = 64
SEL_SHIFT = 6
N_SEL = 16
WINDOW = 512
N_GROUPS = 4
E_PER_GROUP = 4
D_EXPERT = 512
EPS = 1e-5
NEG = -1e30
BIG = 1e9

LANES = 128
SAMPLE_ROW_TILE = 16
VMEM_LIMIT = 56 * 1024 * 1024


def _cparams(sem):
    return pltpu.CompilerParams(dimension_semantics=sem, vmem_limit_bytes=VMEM_LIMIT)


def _modulate_kernel(x_ref, sc_ref, sh_ref, o_ref):
    o_ref[0] = (x_ref[0] * (1.0 + sc_ref[0]) + sh_ref[0]).astype(o_ref.dtype)


def modulate(x, sc, sh, out_dtype):
    B, T, D = x.shape
    tt = min(T, 512)
    return pl.pallas_call(
        _modulate_kernel,
        grid=(B, T // tt),
        in_specs=[pl.BlockSpec((1, tt, D), lambda b, i: (b, i, 0)),
                  pl.BlockSpec((1, 1, D), lambda b, i: (b, 0, 0)),
                  pl.BlockSpec((1, 1, D), lambda b, i: (b, 0, 0))],
        out_specs=pl.BlockSpec((1, tt, D), lambda b, i: (b, i, 0)),
        out_shape=jax.ShapeDtypeStruct((B, T, D), out_dtype),
        compiler_params=_cparams(("parallel", "parallel")),
        name="modulate",
    )(x, sc, sh)


def _mm_kernel(*refs, n_x, has_bias):
    xs = refs[:n_x]
    ws = refs[n_x:2 * n_x]
    o_ref = refs[-1]
    acc = None
    for x_ref, w_ref in zip(xs, ws):
        part = jnp.dot(x_ref[...].astype(BF16), w_ref[...].astype(BF16), preferred_element_type=F32)
        acc = part if acc is None else acc + part
    if has_bias:
        acc = acc + refs[2 * n_x][...]
    o_ref[...] = acc.astype(o_ref.dtype)


def matmul(xs, w, layer, row_offsets, n_cols, *, tm, tn, bias=None, out_dtype=F32, name="matmul"):
    M = xs[0].shape[0]
    tm = min(tm, M)
    assert M % tm == 0 and n_cols % tn == 0
    in_specs = [pl.BlockSpec((tm, x.shape[1]), lambda i, j: (i, 0)) for x in xs]
    for x, off in zip(xs, row_offsets):
        k = x.shape[1]
        assert off % k == 0
        if layer is None:
            in_specs.append(pl.BlockSpec((k, tn), lambda i, j, o=off // k: (o, j)))
        else:
            in_specs.append(pl.BlockSpec((None, k, tn), lambda i, j, o=off // k: (layer, o, j)))
    args = list(xs) + [w] * len(xs)
    if bias is not None:
        in_specs.append(pl.BlockSpec((1, tn), lambda i, j: (0, j)))
        args.append(bias)
    return pl.pallas_call(
        functools.partial(_mm_kernel, n_x=len(xs), has_bias=bias is not None),
        grid=(M // tm, n_cols // tn),
        in_specs=in_specs,
        out_specs=pl.BlockSpec((tm, tn), lambda i, j: (i, j)),
        out_shape=jax.ShapeDtypeStruct((M, n_cols), out_dtype),
        compiler_params=_cparams(("parallel", "arbitrary")),
        name=name,
    )(*args)


def _log_sigmoid(z):
    return jnp.minimum(z, 0.0) - jnp.log1p(jnp.exp(-jnp.abs(z)))


def _sb_prompt_kernel(q_ref, k_ref, v_ref, g_ref, o_ref, *, blk):
    qi = pl.program_id(2)
    scale = HEAD_DIM ** -0.5
    q = q_ref[...].astype(BF16)
    jj = lax.broadcasted_iota(jnp.int32, (blk, blk), 0)
    ss = lax.broadcasted_iota(jnp.int32, (blk, blk), 1)
    later = jnp.where(jj > ss, 1.0, 0.0).astype(BF16)

    def body(i, carry):
        acc, run = carry
        kb = qi - i
        k0 = pl.multiple_of(kb * blk, blk)
        k = k_ref[pl.ds(k0, blk), :].astype(BF16)
        v = v_ref[pl.ds(k0, blk), :].astype(BF16)
        z = lax.dot_general(q, k, (((1,), (1,)), ((), ())), preferred_element_type=F32) * scale
        mask = (kb * blk + ss) < (qi * blk + jj)
        lb = _log_sigmoid(z)
        l1m = jnp.where(mask, lb - z, 0.0)
        hi = l1m.astype(BF16)
        lo = (l1m - hi.astype(F32)).astype(BF16)
        inblk = (jnp.dot(hi, later, preferred_element_type=F32)
                 + jnp.dot(lo, later, preferred_element_type=F32))
        a = jnp.where(mask, jnp.exp(lb + inblk + run), 0.0)
        acc = acc + jnp.dot(a.astype(BF16), v, preferred_element_type=F32)
        run = run + inblk[:, 0:1] + l1m[:, 0:1]
        return acc, run

    acc, _ = lax.fori_loop(0, qi + 1, body,
                           (jnp.zeros((blk, HEAD_DIM), F32), jnp.zeros((blk, 1), F32)))
    ms = jnp.mean(acc * acc, axis=-1, keepdims=True)
    o_ref[...] = (acc * lax.rsqrt(ms + EPS) * g_ref[...]).astype(o_ref.dtype)


def sb_prompt(proj, gain, B, T, *, blk=256):
    blk = min(blk, T)
    nq = T // blk
    return pl.pallas_call(
        functools.partial(_sb_prompt_kernel, blk=blk),
        grid=(B, H_SB, nq),
        in_specs=[pl.BlockSpec((blk, HEAD_DIM), lambda b, h, i: (b * nq + i, h)),
                  pl.BlockSpec((T, HEAD_DIM), lambda b, h, i: (b, H_SB + h)),
                  pl.BlockSpec((T, HEAD_DIM), lambda b, h, i: (b, 2 * H_SB + h)),
                  pl.BlockSpec((1, HEAD_DIM), lambda b, h, i: (0, h))],
        out_specs=pl.BlockSpec((blk, HEAD_DIM), lambda b, h, i: (b * nq + i, h)),
        out_shape=jax.ShapeDtypeStruct((B * T, H_SB * HEAD_DIM), BF16),
        compiler_params=_cparams(("parallel", "parallel", "arbitrary")),
        name="sb_prompt",
    )(proj, proj, proj, gain)


SB_PAGES_PER_STEP = 4
ROWS_PER_TOKEN_SB = 2 * H_SB


def _sb_decode_kernel(pt_ref, qbd_ref, new_ref, tcol_ref, gain_ref, *rest, n_pages_step, page, q_shift):
    page_refs = rest[:n_pages_step]
    o_ref, acc_ref, run_ref = rest[n_pages_step:]
    j = pl.program_id(1)
    scale = HEAD_DIM ** -0.5

    def fold(rows_of, n_keys, masked):
        z = None
        for h in range(H_SB):
            part = jnp.dot(rows_of(0, h).astype(BF16), qbd_ref[0, h * HEAD_DIM:(h + 1) * HEAD_DIM, :].astype(BF16),
                           preferred_element_type=F32)
            z = part if z is None else z + part
        z = z * scale
        lb = _log_sigmoid(z)
        l1m = lb - z
        if masked:
            valid = lax.broadcasted_iota(jnp.int32, (n_keys, LANES), 0) < tcol_ref[...]
            l1m = jnp.where(valid, l1m, 0.0)
        hi = l1m.astype(BF16)
        lo = (l1m - hi.astype(F32)).astype(BF16)
        rs = lax.broadcasted_iota(jnp.int32, (n_keys, n_keys), 0)
        cj = lax.broadcasted_iota(jnp.int32, (n_keys, n_keys), 1)
        later = jnp.where(cj > rs, 1.0, 0.0).astype(BF16)
        inblk = (jnp.dot(later, hi, preferred_element_type=F32) + jnp.dot(later, lo, preferred_element_type=F32))
        a = jnp.exp(lb + inblk + run_ref[...])
        if masked:
            a = jnp.where(valid, a, 0.0)
        run_ref[...] = run_ref[...] + inblk[0:1, :] + l1m[0:1, :]
        at = a.T.astype(BF16)
        for h in range(H_SB):
            acc_ref[h] = acc_ref[h] + jnp.dot(at, rows_of(1, h).astype(BF16), preferred_element_type=F32)

    @pl.when(j == 0)
    def _():
        acc_ref[...] = jnp.zeros(acc_ref.shape, F32)
        run_ref[...] = jnp.zeros(run_ref.shape, F32)
        fold(lambda kv, h: new_ref[0, pl.ds(kv * H_SB + h, LANES, stride=ROWS_PER_TOKEN_SB), :], LANES, True)

    def page_rows(kv, h):
        return jnp.concatenate([r[pl.ds(kv * H_SB + h, page, stride=ROWS_PER_TOKEN_SB), :] for r in page_refs], axis=0)

    fold(page_rows, n_pages_step * page, False)

    @pl.when(j == pl.num_programs(1) - 1)
    def _():
        head_of_row = lax.shift_right_logical(lax.broadcasted_iota(jnp.int32, (LANES, HEAD_DIM), 0), q_shift)
        out = jnp.zeros((LANES, HEAD_DIM), F32)
        for h in range(H_SB):
            out = out + jnp.where(head_of_row == h, acc_ref[h], 0.0)
        ms = jnp.mean(out * out, axis=-1, keepdims=True)
        o_ref[0] = out * lax.rsqrt(ms + EPS) * gain_ref[...]


def sb_decode(q, k_new, v_new, cache, page_table, layer, gain):
    B, Tq, H, hd = q.shape
    depth, n_pool, page = cache.shape[:3]
    n_pages = page_table.shape[1]
    P = SB_PAGES_PER_STEP
    q_shift = Tq.bit_length() - 1
    assert Tq == 1 << q_shift and H * Tq <= LANES and Tq <= LANES and n_pages % P == 0
    eye = jnp.eye(H, dtype=F32)
    qbd = (q.transpose(0, 2, 3, 1)[:, :, :, None, :] * eye[None, :, None, :, None]).reshape(B, H * hd, H * Tq)
    qbd = jnp.pad(qbd, ((0, 0), (0, 0), (0, LANES - H * Tq)))
    new = jnp.stack([k_new, v_new], axis=2).reshape(B, Tq * ROWS_PER_TOKEN_SB, hd)
    new = jnp.pad(new, ((0, 0), (0, (LANES - Tq) * ROWS_PER_TOKEN_SB), (0, 0)))
    lane = jnp.arange(LANES, dtype=jnp.int32)
    tcol = jnp.where(lane < H * Tq, lane % Tq, -1)[None, :]
    gain_rows = jnp.pad(jnp.repeat(gain.reshape(H, hd), Tq, axis=0), ((0, LANES - H * Tq), (0, 0)),
                        constant_values=1.0)
    pages = cache.reshape(depth, n_pool, page * ROWS_PER_TOKEN_SB, hd)
    page_spec = lambda r: pl.BlockSpec(
        (None, None, page * ROWS_PER_TOKEN_SB, hd),
        lambda b, j, pt: (layer, pt[b, n_pages - (j + 1) * P + r], 0, 0))
    out = pl.pallas_call(
        functools.partial(_sb_decode_kernel, n_pages_step=P, page=page, q_shift=q_shift),
        grid_spec=pltpu.PrefetchScalarGridSpec(
            num_scalar_prefetch=1, grid=(B, n_pages // P),
            in_specs=[pl.BlockSpec((1, H * hd, LANES), lambda b, j, pt: (b, 0, 0)),
                      pl.BlockSpec((1, LANES * ROWS_PER_TOKEN_SB, hd), lambda b, j, pt: (b, 0, 0)),
                      pl.BlockSpec((1, LANES), lambda b, j, pt: (0, 0)),
                      pl.BlockSpec((LANES, hd), lambda b, j, pt: (0, 0))] + [page_spec(r) for r in range(P)],
            out_specs=pl.BlockSpec((1, LANES, hd), lambda b, j, pt: (b, 0, 0)),
            scratch_shapes=[pltpu.VMEM((H, LANES, hd), F32), pltpu.VMEM((1, LANES), F32)]),
        out_shape=jax.ShapeDtypeStruct((B, LANES, hd), F32),
        compiler_params=_cparams(("parallel", "arbitrary")),
        name="sb_decode",
    )(page_table, qbd, new, tcol, gain_rows, *([pages] * P))
    return out[:, :H * Tq].reshape(B, H, Tq, hd).transpose(0, 2, 1, 3).reshape(B, Tq, H * hd)


HIST = 32


def _conv_kernel(a_ref, b_ref, hist_ref, w_ref, cb_ref, g_ref, beta_ref, y_ref, st_ref, ext_ref, *, tb, t_valid):
    i = pl.program_id(1)

    @pl.when(i == 0)
    def _():
        ext_ref[0:HIST, :] = hist_ref[0]

    u = a_ref[...] * jax.nn.sigmoid(b_ref[...])
    ext_ref[HIST:HIST + tb, :] = u
    acc = jnp.zeros(u.shape, F32)
    for k in range(CONV_W):
        acc = acc + ext_ref[pl.ds(HIST - (CONV_W - 1) + k, tb), :] * w_ref[k:k + 1, :]
    y = acc + cb_ref[...]
    mu = jnp.mean(y, axis=-1, keepdims=True)
    var = jnp.mean(jnp.square(y - mu), axis=-1, keepdims=True)
    yn = (y - mu) * lax.rsqrt(var + EPS) * g_ref[...] + beta_ref[...]
    y_ref[...] = (yn * jax.nn.sigmoid(yn)).astype(y_ref.dtype)
    st_ref[0] = ext_ref[pl.ds(HIST + t_valid - (CONV_W - 1), CONV_W - 1), :]
    if tb >= HIST:
        ext_ref[0:HIST, :] = ext_ref[tb:tb + HIST, :]


def conformer_conv(proj, hist, conv_w, conv_b, ln_g, ln_b, B, T, t_valid, *, tb=256):
    C = conv_w.shape[-1]
    tb = min(tb, T)
    nt = T // tb
    assert nt == 1 or (tb >= HIST and t_valid == tb)
    return pl.pallas_call(
        functools.partial(_conv_kernel, tb=tb, t_valid=t_valid),
        grid=(B, nt),
        in_specs=[pl.BlockSpec((tb, C), lambda b, i: (b * nt + i, 3)),
                  pl.BlockSpec((tb, C), lambda b, i: (b * nt + i, 4)),
                  pl.BlockSpec((1, HIST, C), lambda b, i: (b, 0, 0)),
                  pl.BlockSpec((CONV_W, C), lambda b, i: (0, 0)),
                  pl.BlockSpec((1, C), lambda b, i: (0, 0)),
                  pl.BlockSpec((1, C), lambda b, i: (0, 0)),
                  pl.BlockSpec((1, C), lambda b, i: (0, 0))],
        out_specs=[pl.BlockSpec((tb, C), lambda b, i: (b * nt + i, 0)),
                   pl.BlockSpec((1, CONV_W - 1, C), lambda b, i: (b, 0, 0))],
        out_shape=[jax.ShapeDtypeStruct((B * T, C), BF16),
                   jax.ShapeDtypeStruct((B, CONV_W - 1, C), F32)],
        scratch_shapes=[pltpu.VMEM((HIST + tb, C), F32)],
        compiler_params=_cparams(("parallel", "arbitrary")),
        name="conformer_conv",
    )(proj, proj, hist, conv_w, conv_b, ln_g, ln_b)


def _compress_kernel(x_ref, w_ref, pe_ref, o_ref, *, n_chunks):
    parts = []
    for r in range(L_CMP // CMP_STRIDE):
        p = jnp.zeros((n_chunks, HEAD_DIM), F32)
        for l in range(CMP_STRIDE):
            j = r * CMP_STRIDE + l
            rows = x_ref[pl.ds(l, n_chunks, stride=CMP_STRIDE), :] + pe_ref[j:j + 1, :]
            p = p + jnp.dot(rows.astype(BF16), w_ref[j].astype(BF16), preferred_element_type=F32)
        parts.append(p)
    o_ref[...] = parts[0] + pltpu.roll(parts[1], n_chunks - 1, 0)


def compress_kv(rows, col_block0, cmp_w, cmp_pe, layer, B, T):
    n_chunks = T // CMP_STRIDE
    return pl.pallas_call(
        functools.partial(_compress_kernel, n_chunks=n_chunks),
        grid=(B, 2, H_KV),
        in_specs=[pl.BlockSpec((T, HEAD_DIM), lambda b, ty, h: (b, col_block0 + ty * H_KV + h)),
                  pl.BlockSpec((None, None, L_CMP, HEAD_DIM, HEAD_DIM), lambda b, ty, h: (layer, ty, 0, 0, 0)),
                  pl.BlockSpec((None, None, L_CMP, HEAD_DIM), lambda b, ty, h: (layer, ty, 0, 0))],
        out_specs=pl.BlockSpec((None, None, None, n_chunks, HEAD_DIM), lambda b, ty, h: (b, ty, h, 0, 0)),
        out_shape=jax.ShapeDtypeStruct((B, 2, H_KV, n_chunks, HEAD_DIM), F32),
        compiler_params=_cparams(("parallel", "parallel", "parallel")),
        name="nsa_compress",
    )(rows, cmp_w, cmp_pe)


def _nsa_prompt_kernel(q_ref, kc_ref, vc_ref, sk_ref, sv_ref, wk_ref, wv_ref, gl_ref, bg_ref, gain_ref,
                       o_ref, *, tq, bk, n_sel):
    qi = pl.program_id(2)
    scale = HEAD_DIM ** -0.5
    rows = G_NSA * tq
    q0 = qi * tq
    q = jnp.concatenate([q_ref[:, g * HEAD_DIM:(g + 1) * HEAD_DIM] for g in range(G_NSA)], axis=0).astype(BF16)
    nt = (((1,), (1,)), ((), ()))

    n_cmp_pad = kc_ref.shape[0]
    s = lax.dot_general(q, kc_ref[...].astype(BF16), nt, preferred_element_type=F32) * scale
    cl = lax.broadcasted_iota(jnp.int32, (rows, n_cmp_pad), 1)
    qp = q0 + (lax.broadcasted_iota(jnp.int32, (rows, n_cmp_pad), 0) & (tq - 1))
    cmask = (cl * CMP_STRIDE + (L_CMP - 1)) <= qp
    sm = jnp.where(cmask, s, NEG)
    e = jnp.where(cmask, jnp.exp(sm - jnp.max(sm, axis=-1, keepdims=True)), 0.0)
    den = jnp.sum(e, axis=-1, keepdims=True)
    p = e / jnp.where(den > 0.0, den, 1.0)
    pb = p.astype(BF16)
    o_cmp = jnp.dot(pb, vc_ref[...].astype(BF16), preferred_element_type=F32)

    cn = lax.broadcasted_iota(jnp.int32, (n_cmp_pad, LANES), 0) * CMP_STRIDE
    sj = lax.broadcasted_iota(jnp.int32, (n_cmp_pad, LANES), 1) * L_SEL
    ov = jnp.clip(jnp.minimum(cn + L_CMP, sj + L_SEL) - jnp.maximum(cn, sj), 0, None).astype(F32) / L_CMP
    ov = ov.astype(BF16)
    imp = None
    for g in range(G_NSA):
        part = jnp.dot(pb[g * tq:(g + 1) * tq], ov, preferred_element_type=F32)
        imp = part if imp is None else imp + part
    blk = lax.broadcasted_iota(jnp.int32, (tq, LANES), 1)
    qpt = q0 + lax.broadcasted_iota(jnp.int32, (tq, LANES), 0)
    cur = jnp.right_shift(qpt, SEL_SHIFT)
    forced = (blk == 0) | (blk == cur) | (blk == cur - 1)
    valid = blk * L_SEL <= qpt
    score = jnp.where(forced, BIG, jnp.where(valid, imp, -BIG))
    score = jnp.where(blk < n_sel, score, -3e38)
    rank = jnp.zeros((tq, LANES), jnp.int32)
    for i in range(n_sel):
        ci = score[:, i:i + 1]
        ahead = (ci > score) | ((ci == score) & (blk > i))
        rank = rank + ahead.astype(jnp.int32)
    sel = jnp.where((rank < min(N_SEL, n_sel)) & (blk < n_sel), 1.0, 0.0).astype(BF16)

    kj = lax.broadcasted_iota(jnp.int32, (LANES, bk), 0)
    ks = lax.broadcasted_iota(jnp.int32, (LANES, bk), 1)
    klane = lax.broadcasted_iota(jnp.int32, (rows, bk), 1)
    qpos = q0 + (lax.broadcasted_iota(jnp.int32, (rows, bk), 0) & (tq - 1))

    def flash_step(k_ref, v_ref, kb, allowed_fn, carry):
        m, l, acc = carry
        k0 = pl.multiple_of(kb * bk, bk)
        k = k_ref[pl.ds(k0, bk), :].astype(BF16)
        v = v_ref[pl.ds(k0, bk), :].astype(BF16)
        sc = lax.dot_general(q, k, nt, preferred_element_type=F32) * scale
        allowed = allowed_fn(kb * bk + klane)
        scm = jnp.where(allowed, sc, NEG)
        m_new = jnp.maximum(m, jnp.max(scm, axis=-1, keepdims=True))
        alpha = jnp.exp(m - m_new)
        pe = jnp.where(allowed, jnp.exp(scm - m_new), 0.0)
        l = alpha * l + jnp.sum(pe, axis=-1, keepdims=True)
        acc = alpha * acc + jnp.dot(pe.astype(BF16), v, preferred_element_type=F32)
        return m_new, l, acc

    init = (jnp.full((rows, 1), NEG, F32), jnp.zeros((rows, 1), F32), jnp.zeros((rows, HEAD_DIM), F32))

    def slc_body(kb, carry):
        expand = jnp.where(jnp.right_shift(kb * bk + ks, SEL_SHIFT) == kj, 1.0, 0.0).astype(BF16)
        selk = jnp.dot(sel, expand, preferred_element_type=F32)
        selk = jnp.concatenate([selk] * G_NSA, axis=0)
        return flash_step(sk_ref, sv_ref, kb, lambda kpos: (selk > 0.5) & (kpos <= qpos), carry)

    last_kb = (q0 + tq - 1) // bk
    _, l_s, acc_s = lax.fori_loop(0, last_kb + 1, slc_body, init)
    o_slc = acc_s / l_s

    def win_body(kb, carry):
        return flash_step(wk_ref, wv_ref, kb,
                          lambda kpos: (qpos - kpos >= 0) & (qpos - kpos < WINDOW), carry)

    first_kb = jnp.maximum(q0 - (WINDOW - 1), 0) // bk
    _, l_w, acc_w = lax.fori_loop(first_kb, last_kb + 1, win_body, init)
    o_win = acc_w / l_w

    gates = jax.nn.sigmoid(gl_ref[...] + bg_ref[...])
    for g in range(G_NSA):
        r = slice(g * tq, (g + 1) * tq)
        o = (gates[:, g:g + 1] * o_cmp[r] + gates[:, G_NSA + g:G_NSA + g + 1] * o_slc[r]
             + gates[:, 2 * G_NSA + g:2 * G_NSA + g + 1] * o_win[r])
        ms = jnp.mean(o * o, axis=-1, keepdims=True)
        o = o * lax.rsqrt(ms + EPS) * gain_ref[:, g * HEAD_DIM:(g + 1) * HEAD_DIM]
        o_ref[:, g * HEAD_DIM:(g + 1) * HEAD_DIM] = o.astype(o_ref.dtype)


def nsa_prompt(proj, kvc, gate_logits, b_gate, gain, B, T):
    tq = LANES
    bk = min(2 * LANES, T)
    assert T % tq == 0 and T % bk == 0 and bk % L_SEL == 0
    nq = T // tq
    n_sel = -(-T // L_SEL)
    assert n_sel <= LANES
    n_cmp_pad = kvc.shape[3]
    qw = G_NSA * HEAD_DIM
    seq = lambda c0: pl.BlockSpec((T, HEAD_DIM), lambda b, h, i: (b, c0 + h))
    return pl.pallas_call(
        functools.partial(_nsa_prompt_kernel, tq=tq, bk=bk, n_sel=n_sel),
        grid=(B, H_KV, nq),
        in_specs=[pl.BlockSpec((tq, qw), lambda b, h, i: (b * nq + i, 5120 // qw + h)),
                  pl.BlockSpec((None, None, None, n_cmp_pad, HEAD_DIM), lambda b, h, i: (b, 0, h, 0, 0)),
                  pl.BlockSpec((None, None, None, n_cmp_pad, HEAD_DIM), lambda b, h, i: (b, 1, h, 0, 0)),
                  seq(64), seq(68), seq(72), seq(76),
                  pl.BlockSpec((tq, LANES), lambda b, h, i: (b * nq + i, h)),
                  pl.BlockSpec((1, LANES), lambda b, h, i: (0, h)),
                  pl.BlockSpec((1, qw), lambda b, h, i: (0, h))],
        out_specs=pl.BlockSpec((tq, qw), lambda b, h, i: (b * nq + i, h)),
        out_shape=jax.ShapeDtypeStruct((B * T, H_KV * qw), BF16),
        compiler_params=_cparams(("parallel", "parallel", "arbitrary")),
        name="nsa_prompt",
    )(proj, kvc, kvc, proj, proj, proj, proj, gate_logits, b_gate, gain)


def _layer_norm(y, g, b):
    mu = jnp.mean(y, axis=-1, keepdims=True)
    var = jnp.mean(jnp.square(y - mu), axis=-1, keepdims=True)
    return (y - mu) * lax.rsqrt(var + EPS) * g + b


def _ln1_kernel(x_ref, h_ref, g1_ref, lg_ref, lb_ref, sc_ref, sh_ref, wr_ref, br_ref, x1_ref, x2_ref, rl_ref, *, alpha):
    x1 = _layer_norm(alpha * x_ref[0] + g1_ref[0] * h_ref[0], lg_ref[...], lb_ref[...])
    x1_ref[0] = x1
    x2 = x1 * (1.0 + sc_ref[0]) + sh_ref[0]
    x2_ref[0] = x2.astype(x2_ref.dtype)
    rl_ref[0] = jnp.dot(x2.astype(BF16), wr_ref[...].astype(BF16), preferred_element_type=F32) + br_ref[...]


def post_mixer(x, h, g1, ln_g, ln_b, sc2, sh2, w_router, b_router, alpha):
    B, T, D = x.shape
    tt = min(T, 256)
    row = pl.BlockSpec((1, tt, D), lambda b, i: (b, i, 0))
    per_b = pl.BlockSpec((1, 1, D), lambda b, i: (b, 0, 0))
    vec = pl.BlockSpec((1, D), lambda b, i: (0, 0))
    return pl.pallas_call(
        functools.partial(_ln1_kernel, alpha=alpha),
        grid=(B, T // tt),
        in_specs=[row, row, per_b, vec, vec, per_b, per_b,
                  pl.BlockSpec((D, LANES), lambda b, i: (0, 0)),
                  pl.BlockSpec((1, LANES), lambda b, i: (0, 0))],
        out_specs=[row, row, pl.BlockSpec((1, tt, LANES), lambda b, i: (b, i, 0))],
        out_shape=[jax.ShapeDtypeStruct((B, T, D), F32), jax.ShapeDtypeStruct((B, T, D), BF16),
                   jax.ShapeDtypeStruct((B, T, LANES), F32)],
        compiler_params=_cparams(("parallel", "parallel")),
        name="post_mixer_ln",
    )(x, h, g1, ln_g, ln_b, sc2, sh2, w_router, b_router)


def _ln2_kernel(x_ref, f_ref, g2_ref, lg_ref, lb_ref, o_ref, *, alpha):
    o_ref[0] = _layer_norm(alpha * x_ref[0] + g2_ref[0] * f_ref[0], lg_ref[...], lb_ref[...])


def post_ffn(x, f, g2, ln_g, ln_b, alpha):
    B, T, D = x.shape
    tt = min(T, 256)
    row = pl.BlockSpec((1, tt, D), lambda b, i: (b, i, 0))
    return pl.pallas_call(
        functools.partial(_ln2_kernel, alpha=alpha),
        grid=(B, T // tt),
        in_specs=[row, row, pl.BlockSpec((1, 1, D), lambda b, i: (b, 0, 0)),
                  pl.BlockSpec((1, D), lambda b, i: (0, 0)), pl.BlockSpec((1, D), lambda b, i: (0, 0))],
        out_specs=row,
        out_shape=jax.ShapeDtypeStruct((B, T, D), F32),
        compiler_params=_cparams(("parallel", "parallel")),
        name="post_ffn_ln",
    )(x, f, g2, ln_g, ln_b)


def _moe_up_kernel(x_ref, wg_ref, wu_ref, cw_ref, h_ref):
    x = x_ref[...]
    hg = jnp.dot(x, wg_ref[...].astype(BF16), preferred_element_type=F32)
    hu = jnp.dot(x, wu_ref[...].astype(BF16), preferred_element_type=F32)
    h_ref[...] = (hg * jax.nn.sigmoid(hg) * hu * cw_ref[...]).astype(h_ref.dtype)


def _mm_acc_kernel(x_ref, w_ref, o_ref, acc_ref):
    k = pl.program_id(2)

    @pl.when(k == 0)
    def _():
        acc_ref[...] = jnp.zeros(acc_ref.shape, F32)

    acc_ref[...] += jnp.dot(x_ref[...].astype(BF16), w_ref[...].astype(BF16), preferred_element_type=F32)

    @pl.when(k == pl.num_programs(2) - 1)
    def _():
        o_ref[...] = acc_ref[...]


def hier_moe(x2, rl, w_gate, w_up, w_down, layer, *, tm):
    N, D = x2.shape
    E = N_GROUPS * E_PER_GROUP
    F = D_EXPERT
    g_prob = jax.nn.softmax(rl[:, :N_GROUPS], axis=-1)
    g_p, g_idx = lax.top_k(g_prob, 1)
    e_logits = rl[:, N_GROUPS:N_GROUPS + E].reshape(N, N_GROUPS, E_PER_GROUP)
    e_logits = jnp.take_along_axis(e_logits, g_idx[:, :, None], axis=1)[:, 0]
    e_p, e_idx = lax.top_k(jax.nn.softmax(e_logits, axis=-1), 2)
    w = g_p * e_p / jnp.sum(e_p, -1, keepdims=True)
    expert = g_idx * E_PER_GROUP + e_idx
    combine = jnp.sum(jax.nn.one_hot(expert, E, dtype=F32) * w[..., None], axis=1)
    tm = min(tm, N)
    fc = F // 2
    per = F // fc
    w_spec = pl.BlockSpec((None, None, D, fc), lambda i, c: (layer, c // per, 0, c % per))
    h = pl.pallas_call(
        _moe_up_kernel,
        grid=(N // tm, E * per),
        in_specs=[pl.BlockSpec((tm, D), lambda i, c: (i, 0)), w_spec, w_spec,
                  pl.BlockSpec((None, tm, 1), lambda i, c: (c // per, i, 0))],
        out_specs=pl.BlockSpec((tm, fc), lambda i, c: (i, c)),
        out_shape=jax.ShapeDtypeStruct((N, E * F), BF16),
        compiler_params=_cparams(("parallel", "arbitrary")),
        name="moe_gate_up",
    )(x2, w_gate, w_up, combine.T[:, :, None])
    wd = w_down.reshape(w_down.shape[0], E * F, D)
    tn = tk = 1024
    return pl.pallas_call(
        _mm_acc_kernel,
        grid=(N // tm, D // tn, E * F // tk),
        in_specs=[pl.BlockSpec((tm, tk), lambda i, j, k: (i, k)),
                  pl.BlockSpec((None, tk, tn), lambda i, j, k: (layer, k, j))],
        out_specs=pl.BlockSpec((tm, tn), lambda i, j, k: (i, j)),
        out_shape=jax.ShapeDtypeStruct((N, D), F32),
        scratch_shapes=[pltpu.VMEM((tm, tn), F32)],
        compiler_params=_cparams(("parallel", "parallel", "arbitrary")),
        name="moe_down",
    )(h, wd)


def _gather_pages(pool, page_table):
    g = pool[page_table]
    return g.reshape((g.shape[0], g.shape[1] * g.shape[2]) + g.shape[3:])


def _head_rms(o, g):
    B, T, H, hd = o.shape
    of = o * lax.rsqrt(jnp.mean(o * o, -1, keepdims=True) + EPS)
    return of.reshape(B, T, H * hd) * g


def _sample_nsa(proj, gate_logits, b_gate, nsa_past, win_buf, kvc, nsa_g, past_len):
    B, T, _ = proj.shape
    hs = lambda c0, h: proj[:, :, c0:c0 + h * HEAD_DIM].reshape(B, T, h, HEAD_DIM)
    q_n = proj[:, :, 5120:7168].reshape(B, T, H_KV, G_NSA, HEAD_DIM)
    sk, sv, wk, wv = hs(8192, H_KV), hs(8704, H_KV), hs(9216, H_KV), hs(9728, H_KV)
    q_pos = past_len + jnp.arange(T)
    kc, vc = kvc[:, 0], kvc[:, 1]
    n_pad = kc.shape[2]
    c_start = jnp.arange(n_pad) * CMP_STRIDE
    cmask = (c_start + (L_CMP - 1))[None, :] <= q_pos[:, None]
    s = jnp.einsum('bqhgd,bhnd->bhgqn', q_n, kc, preferred_element_type=F32) * (HEAD_DIM ** -0.5)
    p = jnp.where(cmask, jax.nn.softmax(jnp.where(cmask, s, NEG), axis=-1), 0.0)
    o_cmp = jnp.einsum('bhgqn,bhnd->bqhgd', p, vc)
    Tk = past_len + T
    n_sel = -(-Tk // L_SEL)
    s_start = jnp.arange(n_sel) * L_SEL
    overlap = jnp.clip(jnp.minimum(c_start[:, None] + L_CMP, s_start[None, :] + L_SEL)
                       - jnp.maximum(c_start[:, None], s_start[None, :]), 0, None).astype(F32) / L_CMP
    imp = jnp.einsum('bhgqn,nj->bhqj', p, overlap)
    cur = q_pos // L_SEL
    blk = jnp.arange(n_sel)
    forced = (blk[None, :] == 0) | (blk[None, :] == cur[:, None]) | (blk[None, :] == cur[:, None] - 1)
    valid = s_start[None, :] <= q_pos[:, None]
    score = jnp.where(forced, BIG, jnp.where(valid, imp, -BIG))
    n_top = min(N_SEL, n_sel)
    _, sel = lax.top_k(score, n_top)
    sk_all = jnp.concatenate([nsa_past[:, :, 2], sk], axis=1)
    sv_all = jnp.concatenate([nsa_past[:, :, 3], sv], axis=1)
    pad = n_sel * L_SEL - Tk
    to_blocks = lambda r: jnp.pad(r, ((0, 0), (0, pad), (0, 0), (0, 0))).reshape(
        B, n_sel, L_SEL, H_KV, HEAD_DIM).transpose(0, 3, 1, 2, 4)
    kb, vb = to_blocks(sk_all), to_blocks(sv_all)
    take = jax.vmap(jax.vmap(lambda blocks, idx: blocks[idx]))
    kg, vg = take(kb, sel), take(vb, sel)
    s2 = jnp.einsum('bqhgd,bhqnld->bhgqnl', q_n, kg, preferred_element_type=F32) * (HEAD_DIM ** -0.5)
    kpos = sel[..., None] * L_SEL + jnp.arange(L_SEL)
    m2 = (kpos <= q_pos[:, None, None])[:, :, None]
    shp = s2.shape
    p2 = jax.nn.softmax(jnp.where(m2, s2, NEG).reshape(shp[:4] + (-1,)), axis=-1).reshape(shp)
    o_slc = jnp.einsum('bhgqnl,bhqnld->bqhgd', p2, vg)
    n_buf = win_buf.shape[1]
    wk_all = jnp.concatenate([win_buf[:, :, 0], wk], axis=1)
    wv_all = jnp.concatenate([win_buf[:, :, 1], wv], axis=1)
    k_pos_w = past_len - n_buf + jnp.arange(n_buf + T)
    s3 = jnp.einsum('bqhgd,bkhd->bhgqk', q_n, wk_all, preferred_element_type=F32) * (HEAD_DIM ** -0.5)
    d = q_pos[:, None] - k_pos_w[None, :]
    m3 = (d >= 0) & (d < WINDOW) & (k_pos_w[None, :] >= 0)
    p3 = jax.nn.softmax(jnp.where(m3, s3, NEG), axis=-1)
    o_win = jnp.einsum('bhgqk,bkhd->bqhgd', p3, wv_all)
    gates = jax.nn.sigmoid(gate_logits + b_gate).reshape(B, T, 3, H_KV, G_NSA, 1)
    o_nsa = gates[:, :, 0] * o_cmp + gates[:, :, 1] * o_slc + gates[:, :, 2] * o_win
    win_state = jnp.concatenate([win_buf, jnp.stack([wk, wv], axis=2)], axis=1)[:, -min(WINDOW, n_buf + T):]
    return _head_rms(o_nsa.reshape(B, T, H_KV * G_NSA, HEAD_DIM), nsa_g), win_state


def _gate_weight(w_in, layer, n_main):
    D = w_in.shape[1]
    wg = w_in[layer, :, n_main:n_main + 3 * H_KV * G_NSA].reshape(D, 3, H_KV, G_NSA)
    wg = wg.transpose(0, 2, 1, 3).reshape(D, H_KV, 3 * G_NSA)
    return jnp.pad(wg, ((0, 0), (0, 0), (0, LANES - 3 * G_NSA))).reshape(D, H_KV * LANES)


def _regroup_gate_bias(b):
    bg = b.reshape(3, H_KV, G_NSA).transpose(1, 0, 2).reshape(H_KV, 3 * G_NSA)
    return jnp.pad(bg, ((0, 0), (0, LANES - 3 * G_NSA))).reshape(1, H_KV * LANES)


def _layer(l, x, mod, past, W, alpha, t_real):
    B, T, D = x.shape
    N = B * T
    n_main = 10240
    sh1, sc1, g1, sh2, sc2, g2 = [m[:, None, :] for m in jnp.split(mod, 6, axis=-1)]
    u = modulate(x, sc1, sh1, BF16).reshape(N, D)
    tm = 1024
    proj = matmul([u], W['w_in'], l, [0], n_main, tm=tm, tn=512, name="in_proj")
    gate_logits = matmul([u], _gate_weight(W['w_in'], l, n_main), None, [0], H_KV * LANES, tm=tm, tn=512,
                         name="gate_proj")
    bg = _regroup_gate_bias(W['b_gate'][l])
    proj3 = proj.reshape(B, T, n_main)[:, :t_real]
    sb_rows = proj3[:, :, 1024:3072].reshape(B, t_real, 2, H_SB, HEAD_DIM)
    nsa_rows = proj3[:, :, 7168:9216].reshape(B, t_real, 4, H_KV, HEAD_DIM)
    win_rows = proj3[:, :, 9216:10240].reshape(B, t_real, 2, H_KV, HEAD_DIM)
    sb_g = W['sb_norm_g'][l][None, :]
    nsa_g = W['nsa_norm_g'][l][None, :]
    conv_args = (W['conv_w'][l], W['conv_b'][l][None, :], W['conv_ln_g'][l][None, :], W['conv_ln_b'][l][None, :])
    if past is None:
        x_sb = sb_prompt(proj, sb_g, B, T)
        hist = jnp.zeros((B, HIST, D // 4), F32)
        x_conv, conv_state = conformer_conv(proj, hist, *conv_args, B, T, min(T, 256))
        kvc = compress_kv(proj, 56, W['cmp_w'], W['cmp_pe'], l, B, T)
        x_nsa = nsa_prompt(proj, kvc, gate_logits, bg, nsa_g, B, T)
        win_state = win_rows[:, -min(WINDOW, T):]
    else:
        cache_sb, page_table, nsa_past, win_buf, conv_buf, past_len = past
        hist = jnp.pad(conv_buf, ((0, 0), (HIST - (CONV_W - 1), 0), (0, 0)))
        x_conv, conv_state = conformer_conv(proj, hist, *conv_args, B, T, t_real)
        assert past_len % CMP_STRIDE == 0 and t_real < CMP_STRIDE
        kvc = compress_kv(nsa_past.reshape(B * past_len, -1), 0, W['cmp_w'], W['cmp_pe'], l, B, past_len)
        glog = gate_logits.reshape(B, T, H_KV, LANES)[:, :t_real, :, :3 * G_NSA].reshape(B, t_real, H_KV, 3, G_NSA)
        glog = glog.transpose(0, 1, 3, 2, 4).reshape(B, t_real, 3 * H_KV * G_NSA)
        o_nsa, win_state = _sample_nsa(proj3, glog, W['b_gate'][l], nsa_past, win_buf, kvc, W['nsa_norm_g'][l],
                                       past_len)
        heads = lambda c0: proj3[:, :, c0:c0 + H_SB * HEAD_DIM].reshape(B, t_real, H_SB, HEAD_DIM)
        o_sb = sb_decode(heads(0), heads(1024), heads(2048), cache_sb, page_table, l, W['sb_norm_g'][l])
        pad_rows = lambda o: jnp.pad(o, ((0, 0), (0, T - t_real), (0, 0))).reshape(N, -1).astype(BF16)
        x_sb, x_nsa = pad_rows(o_sb), pad_rows(o_nsa)
    h = matmul([x_sb, x_conv, x_nsa], W['w_o'], l, [0, 1024, 2048], D, tm=tm, tn=512, name="out_proj")
    w_router = jnp.pad(jnp.concatenate([W['w_rg'][l], W['w_re'][l]], axis=1), ((0, 0), (0, LANES - 20)))
    b_router = jnp.pad(jnp.concatenate([W['b_rg'][l], W['b_re'][l]]), (0, LANES - 20))[None, :]
    x1, x2, rl = post_mixer(x, h.reshape(B, T, D), g1, W['ln1_g'][l][None, :], W['ln1_b'][l][None, :],
                            sc2, sh2, w_router, b_router, alpha)
    f = hier_moe(x2.reshape(N, D), rl.reshape(N, LANES), W['w_gate'], W['w_up'], W['w_down'], l, tm=tm)
    x_out = post_ffn(x1, f.reshape(B, T, D), g2, W['ln2_g'][l][None, :], W['ln2_b'][l][None, :], alpha)
    return x_out, (sb_rows, nsa_rows, win_state, conv_state)


def kernel(x_prompt, x_sample, cache_sb, cache_nsa, state_win, state_conv, page_table, c_prompt, c_sample, w_mod, b_mod, w_in, b_gate, conv_w, conv_b, conv_ln_g, conv_ln_b, cmp_w, cmp_pe, sb_norm_g, nsa_norm_g, w_o, ln1_g, ln1_b, w_rg, b_rg, w_re, b_re, w_gate, w_up, w_down, ln2_g, ln2_b):
    depth = w_mod.shape[0]
    alpha = (2.0 * depth) ** 0.25
    W = dict(w_in=w_in, b_gate=b_gate, conv_w=conv_w, conv_b=conv_b, conv_ln_g=conv_ln_g, conv_ln_b=conv_ln_b,
             cmp_w=cmp_w, cmp_pe=cmp_pe, sb_norm_g=sb_norm_g, nsa_norm_g=nsa_norm_g, w_o=w_o, ln1_g=ln1_g,
             ln1_b=ln1_b, w_rg=w_rg, b_rg=b_rg, w_re=w_re, b_re=b_re, w_gate=w_gate, w_up=w_up, w_down=w_down,
             ln2_g=ln2_g, ln2_b=ln2_b)
    past_len = page_table.shape[1] * cache_sb.shape[2]
    Bp, Bs = c_prompt.shape[0], c_sample.shape[0]
    c_all = jnp.concatenate([c_prompt, c_sample], axis=0)
    c_rows = -(-(Bp + Bs) // 16) * 16
    c_act = jnp.pad(jax.nn.silu(c_all), ((0, c_rows - Bp - Bs), (0, 0)))
    t_s = x_sample.shape[1]
    t_pad = -(-t_s // SAMPLE_ROW_TILE) * SAMPLE_ROW_TILE
    xp = x_prompt
    xs = jnp.pad(x_sample, ((0, 0), (0, t_pad - t_s), (0, 0)))
    st_p, st_s = [], []
    for l in range(depth):
        mod = matmul([c_act], w_mod, l, [0], w_mod.shape[2], tm=c_rows, tn=512, bias=b_mod[l][None, :],
                     name="adaln_mod")
        xp, sp = _layer(l, xp, mod[:Bp], None, W, alpha, xp.shape[1])
        past = (cache_sb, page_table, _gather_pages(cache_nsa[l], page_table), state_win[l], state_conv[l], past_len)
        xs, ss = _layer(l, xs, mod[Bp:Bp + Bs], past, W, alpha, t_s)
        st_p.append(sp)
        st_s.append(ss)
    sb_p, nsa_p, win_p, conv_p = [jnp.stack([s[i] for s in st_p]) for i in range(4)]
    sb_s, nsa_s, win_s, conv_s = [jnp.stack([s[i] for s in st_s]) for i in range(4)]
    return (xp, xs[:, :t_s], sb_p, nsa_p, win_p, conv_p, sb_s, nsa_s, win_s, conv_s)
```

```python
import functools

import jax
import jax.numpy as jnp
import numpy as np
from jax import lax
from jax.experimental import pallas as pl
from jax.experimental.pallas import tpu as pltpu

F32 = jnp.float32
BF16 = jnp.bfloat16

HEAD_DIM = 128
H_SB = 8
H_KV = 4
G_NSA = 4
CONV_W = 31
L_CMP = 32
CMP_STRIDE = 16
L_SEL = 64
SEL_SHIFT = 6
N_SEL = 16
WINDOW = 512
N_GROUPS = 4
E_PER_GROUP = 4
D_EXPERT = 512
EPS = 1e-5
NEG = -1e30
BIG = 1e9

LANES = 128
SAMPLE_ROW_TILE = 16
VMEM_LIMIT = 56 * 1024 * 1024


def _cparams(sem):
    return pltpu.CompilerParams(dimension_semantics=sem, vmem_limit_bytes=VMEM_LIMIT)


def _modulate_kernel(x_ref, sc_ref, sh_ref, o_ref):
    o_ref[0] = (x_ref[0] * (1.0 + sc_ref[0]) + sh_ref[0]).astype(o_ref.dtype)


def modulate(x, sc, sh, out_dtype):
    B, T, D = x.shape
    tt = min(T, 512)
    return pl.pallas_call(
        _modulate_kernel,
        grid=(B, T // tt),
        in_specs=[pl.BlockSpec((1, tt, D), lambda b, i: (b, i, 0)),
                  pl.BlockSpec((1, 1, D), lambda b, i: (b, 0, 0)),
                  pl.BlockSpec((1, 1, D), lambda b, i: (b, 0, 0))],
        out_specs=pl.BlockSpec((1, tt, D), lambda b, i: (b, i, 0)),
        out_shape=jax.ShapeDtypeStruct((B, T, D), out_dtype),
        compiler_params=_cparams(("parallel", "parallel")),
        name="modulate",
    )(x, sc, sh)


def _mm_kernel(*refs, n_x, has_bias):
    xs = refs[:n_x]
    ws = refs[n_x:2 * n_x]
    o_ref = refs[-1]
    acc = None
    for x_ref, w_ref in zip(xs, ws):
        part = jnp.dot(x_ref[...].astype(BF16), w_ref[...].astype(BF16), preferred_element_type=F32)
        acc = part if acc is None else acc + part
    if has_bias:
        acc = acc + refs[2 * n_x][...]
    o_ref[...] = acc.astype(o_ref.dtype)


def matmul(xs, w, layer, row_offsets, n_cols, *, tm, tn, bias=None, out_dtype=F32, name="matmul"):
    M = xs[0].shape[0]
    tm = min(tm, M)
    assert M % tm == 0 and n_cols % tn == 0
    in_specs = [pl.BlockSpec((tm, x.shape[1]), lambda i, j: (i, 0)) for x in xs]
    for x, off in zip(xs, row_offsets):
        k = x.shape[1]
        assert off % k == 0
        if layer is None:
            in_specs.append(pl.BlockSpec((k, tn), lambda i, j, o=off // k: (o, j)))
        else:
            in_specs.append(pl.BlockSpec((None, k, tn), lambda i, j, o=off // k: (layer, o, j)))
    args = list(xs) + [w] * len(xs)
    if bias is not None:
        in_specs.append(pl.BlockSpec((1, tn), lambda i, j: (0, j)))
        args.append(bias)
    return pl.pallas_call(
        functools.partial(_mm_kernel, n_x=len(xs), has_bias=bias is not None),
        grid=(M // tm, n_cols // tn),
        in_specs=in_specs,
        out_specs=pl.BlockSpec((tm, tn), lambda i, j: (i, j)),
        out_shape=jax.ShapeDtypeStruct((M, n_cols), out_dtype),
        compiler_params=_cparams(("parallel", "arbitrary")),
        name=name,
    )(*args)


def _state_rows_kernel(a_ref, b_ref, o_ref, *, tm, half):
    for j in range(half):
        o_ref[pl.ds(j, tm, stride=2 * half), :] = a_ref[:, j * HEAD_DIM:(j + 1) * HEAD_DIM]
        o_ref[pl.ds(half + j, tm, stride=2 * half), :] = b_ref[:, j * HEAD_DIM:(j + 1) * HEAD_DIM]


def state_rows(proj, col0, width):
    N = proj.shape[0]
    wb = width // 2
    half = wb // HEAD_DIM
    assert col0 % wb == 0
    tm = min(256, N)
    return pl.pallas_call(
        functools.partial(_state_rows_kernel, tm=tm, half=half),
        grid=(N // tm,),
        in_specs=[pl.BlockSpec((tm, wb), lambda i: (i, col0 // wb)),
                  pl.BlockSpec((tm, wb), lambda i: (i, col0 // wb + 1))],
        out_specs=pl.BlockSpec((tm * 2 * half, HEAD_DIM), lambda i: (i, 0)),
        out_shape=jax.ShapeDtypeStruct((N * 2 * half, HEAD_DIM), F32),
        compiler_params=_cparams(("parallel",)),
        name="state_rows",
    )(proj, proj)


def _log_sigmoid(z):
    return jnp.minimum(z, 0.0) - jnp.log1p(jnp.exp(-jnp.abs(z)))


def _sb_prompt_kernel(q_ref, k_ref, v_ref, g_ref, o_ref, *, blk):
    qi = pl.program_id(2)
    scale = HEAD_DIM ** -0.5
    q = q_ref[...].astype(BF16)
    jj = lax.broadcasted_iota(jnp.int32, (blk, blk), 0)
    ss = lax.broadcasted_iota(jnp.int32, (blk, blk), 1)
    later = jnp.where(jj > ss, 1.0, 0.0).astype(BF16)

    def body(i, carry):
        acc, run = carry
        kb = qi - i
        k0 = pl.multiple_of(kb * blk, blk)
        k = k_ref[pl.ds(k0, blk), :].astype(BF16)
        v = v_ref[pl.ds(k0, blk), :].astype(BF16)
        z = lax.dot_general(q, k, (((1,), (1,)), ((), ())), preferred_element_type=F32) * scale
        mask = (kb * blk + ss) < (qi * blk + jj)
        lb = _log_sigmoid(z)
        l1m = jnp.where(mask, lb - z, 0.0)
        hi = l1m.astype(BF16)
        lo = (l1m - hi.astype(F32)).astype(BF16)
        inblk = (jnp.dot(hi, later, preferred_element_type=F32)
                 + jnp.dot(lo, later, preferred_element_type=F32))
        a = jnp.where(mask, jnp.exp(lb + inblk + run), 0.0)
        acc = acc + jnp.dot(a.astype(BF16), v, preferred_element_type=F32)
        run = run + inblk[:, 0:1] + l1m[:, 0:1]
        return acc, run

    acc, _ = lax.fori_loop(0, qi + 1, body,
                           (jnp.zeros((blk, HEAD_DIM), F32), jnp.zeros((blk, 1), F32)))
    ms = jnp.mean(acc * acc, axis=-1, keepdims=True)
    o_ref[...] = (acc * lax.rsqrt(ms + EPS) * g_ref[...]).astype(o_ref.dtype)


def sb_prompt(proj, gain, B, T, *, blk=256):
    blk = min(blk, T)
    nq = T // blk
    return pl.pallas_call(
        functools.partial(_sb_prompt_kernel, blk=blk),
        grid=(B, H_SB, nq),
        in_specs=[pl.BlockSpec((blk, HEAD_DIM), lambda b, h, i: (b * nq + i, h)),
                  pl.BlockSpec((T, HEAD_DIM), lambda b, h, i: (b, H_SB + h)),
                  pl.BlockSpec((T, HEAD_DIM), lambda b, h, i: (b, 2 * H_SB + h)),
                  pl.BlockSpec((1, HEAD_DIM), lambda b, h, i: (0, h))],
        out_specs=pl.BlockSpec((blk, HEAD_DIM), lambda b, h, i: (b * nq + i, h)),
        out_shape=jax.ShapeDtypeStruct((B * T, H_SB * HEAD_DIM), BF16),
        compiler_params=_cparams(("parallel", "parallel", "arbitrary")),
        name="sb_prompt",
    )(proj, proj, proj, gain)


SB_PAGES_PER_STEP = 4
ROWS_PER_TOKEN_SB = 2 * H_SB


def _sb_decode_kernel(pt_ref, qbd_ref, new_ref, tcol_ref, gain_ref, *rest, n_pages_step, page, q_shift):
    page_refs = rest[:n_pages_step]
    o_ref, acc_ref, run_ref = rest[n_pages_step:]
    j = pl.program_id(1)
    scale = HEAD_DIM ** -0.5

    def fold(rows_of, n_keys, masked):
        z = None
        for h in range(H_SB):
            part = jnp.dot(rows_of(0, h).astype(BF16), qbd_ref[0, h * HEAD_DIM:(h + 1) * HEAD_DIM, :].astype(BF16),
                           preferred_element_type=F32)
            z = part if z is None else z + part
        z = z * scale
        lb = _log_sigmoid(z)
        l1m = lb - z
        if masked:
            valid = lax.broadcasted_iota(jnp.int32, (n_keys, LANES), 0) < tcol_ref[...]
            l1m = jnp.where(valid, l1m, 0.0)
        hi = l1m.astype(BF16)
        lo = (l1m - hi.astype(F32)).astype(BF16)
        rs = lax.broadcasted_iota(jnp.int32, (n_keys, n_keys), 0)
        cj = lax.broadcasted_iota(jnp.int32, (n_keys, n_keys), 1)
        later = jnp.where(cj > rs, 1.0, 0.0).astype(BF16)
        inblk = (jnp.dot(later, hi, preferred_element_type=F32) + jnp.dot(later, lo, preferred_element_type=F32))
        a = jnp.exp(lb + inblk + run_ref[...])
        if masked:
            a = jnp.where(valid, a, 0.0)
        run_ref[...] = run_ref[...] + inblk[0:1, :] + l1m[0:1, :]
        at = a.T.astype(BF16)
        for h in range(H_SB):
            acc_ref[h] = acc_ref[h] + jnp.dot(at, rows_of(1, h).astype(BF16), preferred_element_type=F32)

    @pl.when(j == 0)
    def _():
        acc_ref[...] = jnp.zeros(acc_ref.shape, F32)
        run_ref[...] = jnp.zeros(run_ref.shape, F32)
        fold(lambda kv, h: new_ref[0, pl.ds(kv * H_SB + h, LANES, stride=ROWS_PER_TOKEN_SB), :], LANES, True)

    def page_rows(kv, h):
        return jnp.concatenate([r[pl.ds(kv * H_SB + h, page, stride=ROWS_PER_TOKEN_SB), :] for r in page_refs], axis=0)

    fold(page_rows, n_pages_step * page, False)

    @pl.when(j == pl.num_programs(1) - 1)
    def _():
        head_of_row = lax.shift_right_logical(lax.broadcasted_iota(jnp.int32, (LANES, HEAD_DIM), 0), q_shift)
        out = jnp.zeros((LANES, HEAD_DIM), F32)
        for h in range(H_SB):
            out = out + jnp.where(head_of_row == h, acc_ref[h], 0.0)
        ms = jnp.mean(out * out, axis=-1, keepdims=True)
        o_ref[0] = out * lax.rsqrt(ms + EPS) * gain_ref[...]


def sb_decode(q, k_new, v_new, cache, page_table, layer, gain):
    B, Tq, H, hd = q.shape
    depth, n_pool, page = cache.shape[:3]
    n_pages = page_table.shape[1]
    P = SB_PAGES_PER_STEP
    q_shift = Tq.bit_length() - 1
    assert Tq == 1 << q_shift and H * Tq <= LANES and Tq <= LANES and n_pages % P == 0
    eye = jnp.eye(H, dtype=F32)
    qbd = (q.transpose(0, 2, 3, 1)[:, :, :, None, :] * eye[None, :, None, :, None]).reshape(B, H * hd, H * Tq)
    qbd = jnp.pad(qbd, ((0, 0), (0, 0), (0, LANES - H * Tq)))
    new = jnp.stack([k_new, v_new], axis=2).reshape(B, Tq * ROWS_PER_TOKEN_SB, hd)
    new = jnp.pad(new, ((0, 0), (0, (LANES - Tq) * ROWS_PER_TOKEN_SB), (0, 0)))
    lane = jnp.arange(LANES, dtype=jnp.int32)
    tcol = jnp.where(lane < H * Tq, lane % Tq, -1)[None, :]
    gain_rows = jnp.pad(jnp.repeat(gain.reshape(H, hd), Tq, axis=0), ((0, LANES - H * Tq), (0, 0)),
                        constant_values=1.0)
    pages = cache.reshape(depth, n_pool, page * ROWS_PER_TOKEN_SB, hd)
    page_spec = lambda r: pl.BlockSpec(
        (None, None, page * ROWS_PER_TOKEN_SB, hd),
        lambda b, j, pt: (layer, pt[b, n_pages - (j + 1) * P + r], 0, 0))
    out = pl.pallas_call(
        functools.partial(_sb_decode_kernel, n_pages_step=P, page=page, q_shift=q_shift),
        grid_spec=pltpu.PrefetchScalarGridSpec(
            num_scalar_prefetch=1, grid=(B, n_pages // P),
            in_specs=[pl.BlockSpec((1, H * hd, LANES), lambda b, j, pt: (b, 0, 0)),
                      pl.BlockSpec((1, LANES * ROWS_PER_TOKEN_SB, hd), lambda b, j, pt: (b, 0, 0)),
                      pl.BlockSpec((1, LANES), lambda b, j, pt: (0, 0)),
                      pl.BlockSpec((LANES, hd), lambda b, j, pt: (0, 0))] + [page_spec(r) for r in range(P)],
            out_specs=pl.BlockSpec((1, LANES, hd), lambda b, j, pt: (b, 0, 0)),
            scratch_shapes=[pltpu.VMEM((H, LANES, hd), F32), pltpu.VMEM((1, LANES), F32)]),
        out_shape=jax.ShapeDtypeStruct((B, LANES, hd), F32),
        compiler_params=_cparams(("parallel", "arbitrary")),
        name="sb_decode",
    )(page_table, qbd, new, tcol, gain_rows, *([pages] * P))
    return out[:, :H * Tq].reshape(B, H, Tq, hd).transpose(0, 2, 1, 3).reshape(B, Tq, H * hd)


HIST = 32


def _conv_kernel(a_ref, b_ref, hist_ref, w_ref, cb_ref, g_ref, beta_ref, y_ref, st_ref, ext_ref, *, tb, t_valid):
    i = pl.program_id(1)

    @pl.when(i == 0)
    def _():
        ext_ref[0:HIST, :] = hist_ref[0]

    u = a_ref[...] * jax.nn.sigmoid(b_ref[...])
    ext_ref[HIST:HIST + tb, :] = u
    acc = jnp.zeros(u.shape, F32)
    for k in range(CONV_W):
        acc = acc + ext_ref[pl.ds(HIST - (CONV_W - 1) + k, tb), :] * w_ref[k:k + 1, :]
    y = acc + cb_ref[...]
    mu = jnp.mean(y, axis=-1, keepdims=True)
    var = jnp.mean(jnp.square(y - mu), axis=-1, keepdims=True)
    yn = (y - mu) * lax.rsqrt(var + EPS) * g_ref[...] + beta_ref[...]
    y_ref[...] = (yn * jax.nn.sigmoid(yn)).astype(y_ref.dtype)
    st_ref[0] = ext_ref[pl.ds(HIST + t_valid - (CONV_W - 1), CONV_W - 1), :]
    if tb >= HIST:
        ext_ref[0:HIST, :] = ext_ref[tb:tb + HIST, :]


def conformer_conv(proj, hist, conv_w, conv_b, ln_g, ln_b, B, T, t_valid, *, tb=256):
    C = conv_w.shape[-1]
    tb = min(tb, T)
    nt = T // tb
    assert nt == 1 or (tb >= HIST and t_valid == tb)
    return pl.pallas_call(
        functools.partial(_conv_kernel, tb=tb, t_valid=t_valid),
        grid=(B, nt),
        in_specs=[pl.BlockSpec((tb, C), lambda b, i: (b * nt + i, 3)),
                  pl.BlockSpec((tb, C), lambda b, i: (b * nt + i, 4)),
                  pl.BlockSpec((1, HIST, C), lambda b, i: (b, 0, 0)),
                  pl.BlockSpec((CONV_W, C), lambda b, i: (0, 0)),
                  pl.BlockSpec((1, C), lambda b, i: (0, 0)),
                  pl.BlockSpec((1, C), lambda b, i: (0, 0)),
                  pl.BlockSpec((1, C), lambda b, i: (0, 0))],
        out_specs=[pl.BlockSpec((tb, C), lambda b, i: (b * nt + i, 0)),
                   pl.BlockSpec((1, CONV_W - 1, C), lambda b, i: (b, 0, 0))],
        out_shape=[jax.ShapeDtypeStruct((B * T, C), BF16),
                   jax.ShapeDtypeStruct((B, CONV_W - 1, C), F32)],
        scratch_shapes=[pltpu.VMEM((HIST + tb, C), F32)],
        compiler_params=_cparams(("parallel", "arbitrary")),
        name="conformer_conv",
    )(proj, proj, hist, conv_w, conv_b, ln_g, ln_b)


def _compress_kernel(x_ref, w_ref, pe_ref, o_ref, *, n_chunks):
    parts = []
    for r in range(L_CMP // CMP_STRIDE):
        p = jnp.zeros((n_chunks, HEAD_DIM), F32)
        for l in range(CMP_STRIDE):
            j = r * CMP_STRIDE + l
            rows = x_ref[pl.ds(l, n_chunks, stride=CMP_STRIDE), :] + pe_ref[j:j + 1, :]
            p = p + jnp.dot(rows.astype(BF16), w_ref[j].astype(BF16), preferred_element_type=F32)
        parts.append(p)
    o_ref[...] = parts[0] + pltpu.roll(parts[1], n_chunks - 1, 0)


def compress_kv(rows, col_block0, cmp_w, cmp_pe, layer, B, T):
    n_chunks = T // CMP_STRIDE
    return pl.pallas_call(
        functools.partial(_compress_kernel, n_chunks=n_chunks),
        grid=(B, 2, H_KV),
        in_specs=[pl.BlockSpec((T, HEAD_DIM), lambda b, ty, h: (b, col_block0 + ty * H_KV + h)),
                  pl.BlockSpec((None, None, L_CMP, HEAD_DIM, HEAD_DIM), lambda b, ty, h: (layer, ty, 0, 0, 0)),
                  pl.BlockSpec((None, None, L_CMP, HEAD_DIM), lambda b, ty, h: (layer, ty, 0, 0))],
        out_specs=pl.BlockSpec((None, None, None, n_chunks, HEAD_DIM), lambda b, ty, h: (b, ty, h, 0, 0)),
        out_shape=jax.ShapeDtypeStruct((B, 2, H_KV, n_chunks, HEAD_DIM), F32),
        compiler_params=_cparams(("parallel", "parallel", "parallel")),
        name="nsa_compress",
    )(rows, cmp_w, cmp_pe)


def _nsa_prompt_kernel(q_ref, kc_ref, vc_ref, sk_ref, sv_ref, wk_ref, wv_ref, gl_ref, bg_ref, gain_ref,
                       o_ref, *, tq, bk, n_sel):
    qi = pl.program_id(2)
    scale = HEAD_DIM ** -0.5
    rows = G_NSA * tq
    q0 = qi * tq
    q = jnp.concatenate([q_ref[:, g * HEAD_DIM:(g + 1) * HEAD_DIM] for g in range(G_NSA)], axis=0).astype(BF16)
    nt = (((1,), (1,)), ((), ()))

    n_cmp_pad = kc_ref.shape[0]
    s = lax.dot_general(q, kc_ref[...].astype(BF16), nt, preferred_element_type=F32) * scale
    cl = lax.broadcasted_iota(jnp.int32, (rows, n_cmp_pad), 1)
    qp = q0 + (lax.broadcasted_iota(jnp.int32, (rows, n_cmp_pad), 0) & (tq - 1))
    cmask = (cl * CMP_STRIDE + (L_CMP - 1)) <= qp
    sm = jnp.where(cmask, s, NEG)
    e = jnp.where(cmask, jnp.exp(sm - jnp.max(sm, axis=-1, keepdims=True)), 0.0)
    den = jnp.sum(e, axis=-1, keepdims=True)
    p = e / jnp.where(den > 0.0, den, 1.0)
    pb = p.astype(BF16)
    o_cmp = jnp.dot(pb, vc_ref[...].astype(BF16), preferred_element_type=F32)

    cn = lax.broadcasted_iota(jnp.int32, (n_cmp_pad, LANES), 0) * CMP_STRIDE
    sj = lax.broadcasted_iota(jnp.int32, (n_cmp_pad, LANES), 1) * L_SEL
    ov = jnp.clip(jnp.minimum(cn + L_CMP, sj + L_SEL) - jnp.maximum(cn, sj), 0, None).astype(F32) / L_CMP
    ov = ov.astype(BF16)
    imp = None
    for g in range(G_NSA):
        part = jnp.dot(pb[g * tq:(g + 1) * tq], ov, preferred_element_type=F32)
        imp = part if imp is None else imp + part
    blk = lax.broadcasted_iota(jnp.int32, (tq, LANES), 1)
    qpt = q0 + lax.broadcasted_iota(jnp.int32, (tq, LANES), 0)
    cur = jnp.right_shift(qpt, SEL_SHIFT)
    forced = (blk == 0) | (blk == cur) | (blk == cur - 1)
    valid = blk * L_SEL <= qpt
    score = jnp.where(forced, BIG, jnp.where(valid, imp, -BIG))
    score = jnp.where(blk < n_sel, score, -3e38)
    rank = jnp.zeros((tq, LANES), jnp.int32)
    for i in range(n_sel):
        ci = score[:, i:i + 1]
        ahead = (ci > score) | ((ci == score) & (blk > i))
        rank = rank + ahead.astype(jnp.int32)
    sel = jnp.where((rank < min(N_SEL, n_sel)) & (blk < n_sel), 1.0, 0.0).astype(BF16)

    kj = lax.broadcasted_iota(jnp.int32, (LANES, bk), 0)
    ks = lax.broadcasted_iota(jnp.int32, (LANES, bk), 1)
    klane = lax.broadcasted_iota(jnp.int32, (rows, bk), 1)
    qpos = q0 + (lax.broadcasted_iota(jnp.int32, (rows, bk), 0) & (tq - 1))

    def flash_step(k_ref, v_ref, kb, allowed_fn, carry):
        m, l, acc = carry
        k0 = pl.multiple_of(kb * bk, bk)
        k = k_ref[pl.ds(k0, bk), :].astype(BF16)
        v = v_ref[pl.ds(k0, bk), :].astype(BF16)
        sc = lax.dot_general(q, k, nt, preferred_element_type=F32) * scale
        allowed = allowed_fn(kb * bk + klane)
        scm = jnp.where(allowed, sc, NEG)
        m_new = jnp.maximum(m, jnp.max(scm, axis=-1, keepdims=True))
        alpha = jnp.exp(m - m_new)
        pe = jnp.where(allowed, jnp.exp(scm - m_new), 0.0)
        l = alpha * l + jnp.sum(pe, axis=-1, keepdims=True)
        acc = alpha * acc + jnp.dot(pe.astype(BF16), v, preferred_element_type=F32)
        return m_new, l, acc

    init = (jnp.full((rows, 1), NEG, F32), jnp.zeros((rows, 1), F32), jnp.zeros((rows, HEAD_DIM), F32))

    def slc_body(kb, carry):
        expand = jnp.where(jnp.right_shift(kb * bk + ks, SEL_SHIFT) == kj, 1.0, 0.0).astype(BF16)
        selk = jnp.dot(sel, expand, preferred_element_type=F32)
        selk = jnp.concatenate([selk] * G_NSA, axis=0)
        return flash_step(sk_ref, sv_ref, kb, lambda kpos: (selk > 0.5) & (kpos <= qpos), carry)

    last_kb = (q0 + tq - 1) // bk
    _, l_s, acc_s = lax.fori_loop(0, last_kb + 1, slc_body, init)
    o_slc = acc_s / l_s

    def win_body(kb, carry):
        return flash_step(wk_ref, wv_ref, kb,
                          lambda kpos: (qpos - kpos >= 0) & (qpos - kpos < WINDOW), carry)

    first_kb = jnp.maximum(q0 - (WINDOW - 1), 0) // bk
    _, l_w, acc_w = lax.fori_loop(first_kb, last_kb + 1, win_body, init)
    o_win = acc_w / l_w

    gates = jax.nn.sigmoid(gl_ref[...] + bg_ref[...])
    for g in range(G_NSA):
        r = slice(g * tq, (g + 1) * tq)
        o = (gates[:, g:g + 1] * o_cmp[r] + gates[:, G_NSA + g:G_NSA + g + 1] * o_slc[r]
             + gates[:, 2 * G_NSA + g:2 * G_NSA + g + 1] * o_win[r])
        ms = jnp.mean(o * o, axis=-1, keepdims=True)
        o = o * lax.rsqrt(ms + EPS) * gain_ref[:, g * HEAD_DIM:(g + 1) * HEAD_DIM]
        o_ref[:, g * HEAD_DIM:(g + 1) * HEAD_DIM] = o.astype(o_ref.dtype)


def nsa_prompt(proj, kvc, gate_logits, b_gate, gain, B, T):
    tq = LANES
    bk = min(2 * LANES, T)
    assert T % tq == 0 and T % bk == 0 and bk % L_SEL == 0
    nq = T // tq
    n_sel = -(-T // L_SEL)
    assert n_sel <= LANES
    n_cmp_pad = kvc.shape[3]
    qw = G_NSA * HEAD_DIM
    seq = lambda c0: pl.BlockSpec((T, HEAD_DIM), lambda b, h, i: (b, c0 + h))
    return pl.pallas_call(
        functools.partial(_nsa_prompt_kernel, tq=tq, bk=bk, n_sel=n_sel),
        grid=(B, H_KV, nq),
        in_specs=[pl.BlockSpec((tq, qw), lambda b, h, i: (b * nq + i, 5120 // qw + h)),
                  pl.BlockSpec((None, None, None, n_cmp_pad, HEAD_DIM), lambda b, h, i: (b, 0, h, 0, 0)),
                  pl.BlockSpec((None, None, None, n_cmp_pad, HEAD_DIM), lambda b, h, i: (b, 1, h, 0, 0)),
                  seq(64), seq(68), seq(72), seq(76),
                  pl.BlockSpec((tq, LANES), lambda b, h, i: (b * nq + i, h)),
                  pl.BlockSpec((1, LANES), lambda b, h, i: (0, h)),
                  pl.BlockSpec((1, qw), lambda b, h, i: (0, h))],
        out_specs=pl.BlockSpec((tq, qw), lambda b, h, i: (b * nq + i, h)),
        out_shape=jax.ShapeDtypeStruct((B * T, H_KV * qw), BF16),
        compiler_params=_cparams(("parallel", "parallel", "arbitrary")),
        name="nsa_prompt",
    )(proj, kvc, kvc, proj, proj, proj, proj, gate_logits, b_gate, gain)


def _layer_norm(y, g, b):
    mu = jnp.mean(y, axis=-1, keepdims=True)
    var = jnp.mean(jnp.square(y - mu), axis=-1, keepdims=True)
    return (y - mu) * lax.rsqrt(var + EPS) * g + b


def _ln1_kernel(x_ref, h_ref, g1_ref, lg_ref, lb_ref, sc_ref, sh_ref, wr_ref, br_ref, x1_ref, x2_ref, rl_ref, *, alpha):
    x1 = _layer_norm(alpha * x_ref[0] + g1_ref[0] * h_ref[0], lg_ref[...], lb_ref[...])
    x1_ref[0] = x1
    x2 = x1 * (1.0 + sc_ref[0]) + sh_ref[0]
    x2_ref[0] = x2.astype(x2_ref.dtype)
    rl_ref[0] = jnp.dot(x2.astype(BF16), wr_ref[...].astype(BF16), preferred_element_type=F32) + br_ref[...]


def post_mixer(x, h, g1, ln_g, ln_b, sc2, sh2, w_router, b_router, alpha):
    B, T, D = x.shape
    tt = min(T, 256)
    row = pl.BlockSpec((1, tt, D), lambda b, i: (b, i, 0))
    per_b = pl.BlockSpec((1, 1, D), lambda b, i: (b, 0, 0))
    vec = pl.BlockSpec((1, D), lambda b, i: (0, 0))
    return pl.pallas_call(
        functools.partial(_ln1_kernel, alpha=alpha),
        grid=(B, T // tt),
        in_specs=[row, row, per_b, vec, vec, per_b, per_b,
                  pl.BlockSpec((D, LANES), lambda b, i: (0, 0)),
                  pl.BlockSpec((1, LANES), lambda b, i: (0, 0))],
        out_specs=[row, row, pl.BlockSpec((1, tt, LANES), lambda b, i: (b, i, 0))],
        out_shape=[jax.ShapeDtypeStruct((B, T, D), F32), jax.ShapeDtypeStruct((B, T, D), BF16),
                   jax.ShapeDtypeStruct((B, T, LANES), F32)],
        compiler_params=_cparams(("parallel", "parallel")),
        name="post_mixer_ln",
    )(x, h, g1, ln_g, ln_b, sc2, sh2, w_router, b_router)


def _ln2_kernel(x_ref, f_ref, g2_ref, lg_ref, lb_ref, o_ref, *, alpha):
    o_ref[0] = _layer_norm(alpha * x_ref[0] + g2_ref[0] * f_ref[0], lg_ref[...], lb_ref[...])


def post_ffn(x, f, g2, ln_g, ln_b, alpha):
    B, T, D = x.shape
    tt = min(T, 256)
    row = pl.BlockSpec((1, tt, D), lambda b, i: (b, i, 0))
    return pl.pallas_call(
        functools.partial(_ln2_kernel, alpha=alpha),
        grid=(B, T // tt),
        in_specs=[row, row, pl.BlockSpec((1, 1, D), lambda b, i: (b, 0, 0)),
                  pl.BlockSpec((1, D), lambda b, i: (0, 0)), pl.BlockSpec((1, D), lambda b, i: (0, 0))],
        out_specs=row,
        out_shape=jax.ShapeDtypeStruct((B, T, D), F32),
        compiler_params=_cparams(("parallel", "parallel")),
        name="post_ffn_ln",
    )(x, f, g2, ln_g, ln_b)


def _moe_up_kernel(x_ref, wg_ref, wu_ref, cw_ref, h_ref):
    x = x_ref[...]
    hg = jnp.dot(x, wg_ref[...].astype(BF16), preferred_element_type=F32)
    hu = jnp.dot(x, wu_ref[...].astype(BF16), preferred_element_type=F32)
    h_ref[...] = (hg * jax.nn.sigmoid(hg) * hu * cw_ref[...]).astype(h_ref.dtype)


def _mm_acc_kernel(x_ref, w_ref, o_ref, acc_ref):
    k = pl.program_id(2)

    @pl.when(k == 0)
    def _():
        acc_ref[...] = jnp.zeros(acc_ref.shape, F32)

    acc_ref[...] += jnp.dot(x_ref[...].astype(BF16), w_ref[...].astype(BF16), preferred_element_type=F32)

    @pl.when(k == pl.num_programs(2) - 1)
    def _():
        o_ref[...] = acc_ref[...]


def hier_moe(x2, rl, w_gate, w_up, w_down, layer, *, tm):
    N, D = x2.shape
    E = N_GROUPS * E_PER_GROUP
    F = D_EXPERT
    g_prob = jax.nn.softmax(rl[:, :N_GROUPS], axis=-1)
    g_p, g_idx = lax.top_k(g_prob, 1)
    e_logits = rl[:, N_GROUPS:N_GROUPS + E].reshape(N, N_GROUPS, E_PER_GROUP)
    e_logits = jnp.take_along_axis(e_logits, g_idx[:, :, None], axis=1)[:, 0]
    e_p, e_idx = lax.top_k(jax.nn.softmax(e_logits, axis=-1), 2)
    w = g_p * e_p / jnp.sum(e_p, -1, keepdims=True)
    expert = g_idx * E_PER_GROUP + e_idx
    combine = jnp.sum(jax.nn.one_hot(expert, E, dtype=F32) * w[..., None], axis=1)
    tm = min(tm, N)
    fc = F // 2
    per = F // fc
    w_spec = pl.BlockSpec((None, None, D, fc), lambda i, c: (layer, c // per, 0, c % per))
    h = pl.pallas_call(
        _moe_up_kernel,
        grid=(N // tm, E * per),
        in_specs=[pl.BlockSpec((tm, D), lambda i, c: (i, 0)), w_spec, w_spec,
                  pl.BlockSpec((None, tm, 1), lambda i, c: (c // per, i, 0))],
        out_specs=pl.BlockSpec((tm, fc), lambda i, c: (i, c)),
        out_shape=jax.ShapeDtypeStruct((N, E * F), BF16),
        compiler_params=_cparams(("parallel", "arbitrary")),
        name="moe_gate_up",
    )(x2, w_gate, w_up, combine.T[:, :, None])
    wd = w_down.reshape(w_down.shape[0], E * F, D)
    tn = tk = 1024
    return pl.pallas_call(
        _mm_acc_kernel,
        grid=(N // tm, D // tn, E * F // tk),
        in_specs=[pl.BlockSpec((tm, tk), lambda i, j, k: (i, k)),
                  pl.BlockSpec((None, tk, tn), lambda i, j, k: (layer, k, j))],
        out_specs=pl.BlockSpec((tm, tn), lambda i, j, k: (i, j)),
        out_shape=jax.ShapeDtypeStruct((N, D), F32),
        scratch_shapes=[pltpu.VMEM((tm, tn), F32)],
        compiler_params=_cparams(("parallel", "parallel", "arbitrary")),
        name="moe_down",
    )(h, wd)


def _gather_pages(pool, page_table):
    g = pool[page_table]
    return g.reshape((g.shape[0], g.shape[1] * g.shape[2]) + g.shape[3:])


def _head_rms(o, g):
    B, T, H, hd = o.shape
    of = o * lax.rsqrt(jnp.mean(o * o, -1, keepdims=True) + EPS)
    return of.reshape(B, T, H * hd) * g


def _sample_nsa(proj, gate_logits, b_gate, nsa_past, win_buf, kvc, nsa_g, past_len):
    B, T, _ = proj.shape
    hs = lambda c0, h: proj[:, :, c0:c0 + h * HEAD_DIM].reshape(B, T, h, HEAD_DIM)
    q_n = proj[:, :, 5120:7168].reshape(B, T, H_KV, G_NSA, HEAD_DIM)
    sk, sv, wk, wv = hs(8192, H_KV), hs(8704, H_KV), hs(9216, H_KV), hs(9728, H_KV)
    q_pos = past_len + jnp.arange(T)
    kc, vc = kvc[:, 0], kvc[:, 1]
    n_pad = kc.shape[2]
    c_start = jnp.arange(n_pad) * CMP_STRIDE
    cmask = (c_start + (L_CMP - 1))[None, :] <= q_pos[:, None]
    s = jnp.einsum('bqhgd,bhnd->bhgqn', q_n, kc, preferred_element_type=F32) * (HEAD_DIM ** -0.5)
    p = jnp.where(cmask, jax.nn.softmax(jnp.where(cmask, s, NEG), axis=-1), 0.0)
    o_cmp = jnp.einsum('bhgqn,bhnd->bqhgd', p, vc)
    Tk = past_len + T
    n_sel = -(-Tk // L_SEL)
    s_start = jnp.arange(n_sel) * L_SEL
    overlap = jnp.clip(jnp.minimum(c_start[:, None] + L_CMP, s_start[None, :] + L_SEL)
                       - jnp.maximum(c_start[:, None], s_start[None, :]), 0, None).astype(F32) / L_CMP
    imp = jnp.einsum('bhgqn,nj->bhqj', p, overlap)
    cur = q_pos // L_SEL
    blk = jnp.arange(n_sel)
    forced = (blk[None, :] == 0) | (blk[None, :] == cur[:, None]) | (blk[None, :] == cur[:, None] - 1)
    valid = s_start[None, :] <= q_pos[:, None]
    score = jnp.where(forced, BIG, jnp.where(valid, imp, -BIG))
    n_top = min(N_SEL, n_sel)
    _, sel = lax.top_k(score, n_top)
    sk_all = jnp.concatenate([nsa_past[:, :, 2], sk], axis=1)
    sv_all = jnp.concatenate([nsa_past[:, :, 3], sv], axis=1)
    pad = n_sel * L_SEL - Tk
    to_blocks = lambda r: jnp.pad(r, ((0, 0), (0, pad), (0, 0), (0, 0))).reshape(
        B, n_sel, L_SEL, H_KV, HEAD_DIM).transpose(0, 3, 1, 2, 4)
    kb, vb = to_blocks(sk_all), to_blocks(sv_all)
    take = jax.vmap(jax.vmap(lambda blocks, idx: blocks[idx]))
    kg, vg = take(kb, sel), take(vb, sel)
    s2 = jnp.einsum('bqhgd,bhqnld->bhgqnl', q_n, kg, preferred_element_type=F32) * (HEAD_DIM ** -0.5)
    kpos = sel[..., None] * L_SEL + jnp.arange(L_SEL)
    m2 = (kpos <= q_pos[:, None, None])[:, :, None]
    shp = s2.shape
    p2 = jax.nn.softmax(jnp.where(m2, s2, NEG).reshape(shp[:4] + (-1,)), axis=-1).reshape(shp)
    o_slc = jnp.einsum('bhgqnl,bhqnld->bqhgd', p2, vg)
    n_buf = win_buf.shape[1]
    wk_all = jnp.concatenate([win_buf[:, :, 0], wk], axis=1)
    wv_all = jnp.concatenate([win_buf[:, :, 1], wv], axis=1)
    k_pos_w = past_len - n_buf + jnp.arange(n_buf + T)
    s3 = jnp.einsum('bqhgd,bkhd->bhgqk', q_n, wk_all, preferred_element_type=F32) * (HEAD_DIM ** -0.5)
    d = q_pos[:, None] - k_pos_w[None, :]
    m3 = (d >= 0) & (d < WINDOW) & (k_pos_w[None, :] >= 0)
    p3 = jax.nn.softmax(jnp.where(m3, s3, NEG), axis=-1)
    o_win = jnp.einsum('bhgqk,bkhd->bqhgd', p3, wv_all)
    gates = jax.nn.sigmoid(gate_logits + b_gate).reshape(B, T, 3, H_KV, G_NSA, 1)
    o_nsa = gates[:, :, 0] * o_cmp + gates[:, :, 1] * o_slc + gates[:, :, 2] * o_win
    win_state = jnp.concatenate([win_buf, jnp.stack([wk, wv], axis=2)], axis=1)[:, -min(WINDOW, n_buf + T):]
    return _head_rms(o_nsa.reshape(B, T, H_KV * G_NSA, HEAD_DIM), nsa_g), win_state


def _gate_weight(w_in, layer, n_main):
    D = w_in.shape[1]
    wg = w_in[layer, :, n_main:n_main + 3 * H_KV * G_NSA].reshape(D, 3, H_KV, G_NSA)
    wg = wg.transpose(0, 2, 1, 3).reshape(D, H_KV, 3 * G_NSA)
    return jnp.pad(wg, ((0, 0), (0, 0), (0, LANES - 3 * G_NSA))).reshape(D, H_KV * LANES)


def _regroup_gate_bias(b):
    bg = b.reshape(3, H_KV, G_NSA).transpose(1, 0, 2).reshape(H_KV, 3 * G_NSA)
    return jnp.pad(bg, ((0, 0), (0, LANES - 3 * G_NSA))).reshape(1, H_KV * LANES)


def _layer(l, x, mod, past, W, alpha, t_real):
    B, T, D = x.shape
    N = B * T
    n_main = 10240
    sh1, sc1, g1, sh2, sc2, g2 = [m[:, None, :] for m in jnp.split(mod, 6, axis=-1)]
    u = modulate(x, sc1, sh1, BF16).reshape(N, D)
    tm = 1024
    proj = matmul([u], W['w_in'], l, [0], n_main, tm=tm, tn=512, name="in_proj")
    gate_logits = matmul([u], _gate_weight(W['w_in'], l, n_main), None, [0], H_KV * LANES, tm=tm, tn=512,
                         name="gate_proj")
    bg = _regroup_gate_bias(W['b_gate'][l])
    proj3 = proj.reshape(B, T, n_main)[:, :t_real]
    if past is None:
        sb_rows = state_rows(proj, 1024, 2048).reshape(B, T, 2, H_SB, HEAD_DIM)
        nsa_rows = state_rows(proj, 7168, 2048).reshape(B, T, 4, H_KV, HEAD_DIM)
    else:
        sb_rows = proj3[:, :, 1024:3072].reshape(B, t_real, 2, H_SB, HEAD_DIM)
        nsa_rows = proj3[:, :, 7168:9216].reshape(B, t_real, 4, H_KV, HEAD_DIM)
    win_rows = proj3[:, :, 9216:10240].reshape(B, t_real, 2, H_KV, HEAD_DIM)
    sb_g = W['sb_norm_g'][l][None, :]
    nsa_g = W['nsa_norm_g'][l][None, :]
    conv_args = (W['conv_w'][l], W['conv_b'][l][None, :], W['conv_ln_g'][l][None, :], W['conv_ln_b'][l][None, :])
    if past is None:
        x_sb = sb_prompt(proj, sb_g, B, T)
        hist = jnp.zeros((B, HIST, D // 4), F32)
        x_conv, conv_state = conformer_conv(proj, hist, *conv_args, B, T, min(T, 256))
        kvc = compress_kv(proj, 56, W['cmp_w'], W['cmp_pe'], l, B, T)
        x_nsa = nsa_prompt(proj, kvc, gate_logits, bg, nsa_g, B, T)
        win_state = win_rows[:, -min(WINDOW, T):]
    else:
        cache_sb, page_table, nsa_past, win_buf, conv_buf, past_len = past
        hist = jnp.pad(conv_buf, ((0, 0), (HIST - (CONV_W - 1), 0), (0, 0)))
        x_conv, conv_state = conformer_conv(proj, hist, *conv_args, B, T, t_real)
        assert past_len % CMP_STRIDE == 0 and t_real < CMP_STRIDE
        kvc = compress_kv(nsa_past.reshape(B * past_len, -1), 0, W['cmp_w'], W['cmp_pe'], l, B, past_len)
        glog = gate_logits.reshape(B, T, H_KV, LANES)[:, :t_real, :, :3 * G_NSA].reshape(B, t_real, H_KV, 3, G_NSA)
        glog = glog.transpose(0, 1, 3, 2, 4).reshape(B, t_real, 3 * H_KV * G_NSA)
        o_nsa, win_state = _sample_nsa(proj3, glog, W['b_gate'][l], nsa_past, win_buf, kvc, W['nsa_norm_g'][l],
                                       past_len)
        heads = lambda c0: proj3[:, :, c0:c0 + H_SB * HEAD_DIM].reshape(B, t_real, H_SB, HEAD_DIM)
        o_sb = sb_decode(heads(0), heads(1024), heads(2048), cache_sb, page_table, l, W['sb_norm_g'][l])
        pad_rows = lambda o: jnp.pad(o, ((0, 0), (0, T - t_real), (0, 0))).reshape(N, -1).astype(BF16)
        x_sb, x_nsa = pad_rows(o_sb), pad_rows(o_nsa)
    h = matmul([x_sb, x_conv, x_nsa], W['w_o'], l, [0, 1024, 2048], D, tm=tm, tn=512, name="out_proj")
    w_router = jnp.pad(jnp.concatenate([W['w_rg'][l], W['w_re'][l]], axis=1), ((0, 0), (0, LANES - 20)))
    b_router = jnp.pad(jnp.concatenate([W['b_rg'][l], W['b_re'][l]]), (0, LANES - 20))[None, :]
    x1, x2, rl = post_mixer(x, h.reshape(B, T, D), g1, W['ln1_g'][l][None, :], W['ln1_b'][l][None, :],
                            sc2, sh2, w_router, b_router, alpha)
    f = hier_moe(x2.reshape(N, D), rl.reshape(N, LANES), W['w_gate'], W['w_up'], W['w_down'], l, tm=tm)
    x_out = post_ffn(x1, f.reshape(B, T, D), g2, W['ln2_g'][l][None, :], W['ln2_b'][l][None, :], alpha)
    return x_out, (sb_rows, nsa_rows, win_state, conv_state)


def kernel(x_prompt, x_sample, cache_sb, cache_nsa, state_win, state_conv, page_table, c_prompt, c_sample, w_mod, b_mod, w_in, b_gate, conv_w, conv_b, conv_ln_g, conv_ln_b, cmp_w, cmp_pe, sb_norm_g, nsa_norm_g, w_o, ln1_g, ln1_b, w_rg, b_rg, w_re, b_re, w_gate, w_up, w_down, ln2_g, ln2_b):
    depth = w_mod.shape[0]
    alpha = (2.0 * depth) ** 0.25
    W = dict(w_in=w_in, b_gate=b_gate, conv_w=conv_w, conv_b=conv_b, conv_ln_g=conv_ln_g, conv_ln_b=conv_ln_b,
             cmp_w=cmp_w, cmp_pe=cmp_pe, sb_norm_g=sb_norm_g, nsa_norm_g=nsa_norm_g, w_o=w_o, ln1_g=ln1_g,
             ln1_b=ln1_b, w_rg=w_rg, b_rg=b_rg, w_re=w_re, b_re=b_re, w_gate=w_gate, w_up=w_up, w_down=w_down,
             ln2_g=ln2_g, ln2_b=ln2_b)
    past_len = page_table.shape[1] * cache_sb.shape[2]
    Bp, Bs = c_prompt.shape[0], c_sample.shape[0]
    c_all = jnp.concatenate([c_prompt, c_sample], axis=0)
    c_rows = -(-(Bp + Bs) // 16) * 16
    c_act = jnp.pad(jax.nn.silu(c_all), ((0, c_rows - Bp - Bs), (0, 0)))
    t_s = x_sample.shape[1]
    t_pad = -(-t_s // SAMPLE_ROW_TILE) * SAMPLE_ROW_TILE
    xp = x_prompt
    xs = jnp.pad(x_sample, ((0, 0), (0, t_pad - t_s), (0, 0)))
    st_p, st_s = [], []
    for l in range(depth):
        mod = matmul([c_act], w_mod, l, [0], w_mod.shape[2], tm=c_rows, tn=512, bias=b_mod[l][None, :],
                     name="adaln_mod")
        xp, sp = _layer(l, xp, mod[:Bp], None, W, alpha, xp.shape[1])
        past = (cache_sb, page_table, _gather_pages(cache_nsa[l], page_table), state_win[l], state_conv[l], past_len)
        xs, ss = _layer(l, xs, mod[Bp:Bp + Bs], past, W, alpha, t_s)
        st_p.append(sp)
        st_s.append(ss)
    sb_p, nsa_p, win_p, conv_p = [jnp.stack([s[i] for s in st_p]) for i in range(4)]
    sb_s, nsa_s, win_s, conv_s = [jnp.stack([s[i] for s in st_s]) for i in range(4)]
    return (xp, xs[:, :t_s], sb_p, nsa_p, win_p, conv_p, sb_s, nsa_s, win_s, conv_s)
```

```python
import functools

import jax
import jax.numpy as jnp
import numpy as np
from jax import lax
from jax.experimental import pallas as pl
from jax.experimental.pallas import tpu as pltpu

F32 = jnp.float32
BF16 = jnp.bfloat16

HEAD_DIM = 128
H_SB = 8
H_KV = 4
G_NSA = 4
CONV_W = 31
L_CMP = 32
CMP_STRIDE = 16
L_SEL = 64
SEL_SHIFT = 6
N_SEL = 16
WINDOW = 512
N_GROUPS = 4
E_PER_GROUP = 4
D_EXPERT = 512
EPS = 1e-5
NEG = -1e30
BIG = 1e9

LANES = 128
SAMPLE_ROW_TILE = 16
VMEM_LIMIT = 56 * 1024 * 1024


def _cparams(sem):
    return pltpu.CompilerParams(dimension_semantics=sem, vmem_limit_bytes=VMEM_LIMIT)


def _modulate_kernel(x_ref, sc_ref, sh_ref, o_ref):
    o_ref[0] = (x_ref[0] * (1.0 + sc_ref[0]) + sh_ref[0]).astype(o_ref.dtype)


def modulate(x, sc, sh, out_dtype):
    B, T, D = x.shape
    tt = min(T, 512)
    return pl.pallas_call(
        _modulate_kernel,
        grid=(B, T // tt),
        in_specs=[pl.BlockSpec((1, tt, D), lambda b, i: (b, i, 0)),
                  pl.BlockSpec((1, 1, D), lambda b, i: (b, 0, 0)),
                  pl.BlockSpec((1, 1, D), lambda b, i: (b, 0, 0))],
        out_specs=pl.BlockSpec((1, tt, D), lambda b, i: (b, i, 0)),
        out_shape=jax.ShapeDtypeStruct((B, T, D), out_dtype),
        compiler_params=_cparams(("parallel", "parallel")),
        name="modulate",
    )(x, sc, sh)


def _mm_kernel(*refs, n_x, has_bias):
    xs = refs[:n_x]
    ws = refs[n_x:2 * n_x]
    o_ref = refs[-1]
    acc = None
    for x_ref, w_ref in zip(xs, ws):
        part = jnp.dot(x_ref[...].astype(BF16), w_ref[...].astype(BF16), preferred_element_type=F32)
        acc = part if acc is None else acc + part
    if has_bias:
        acc = acc + refs[2 * n_x][...]
    o_ref[...] = acc.astype(o_ref.dtype)


def matmul(xs, w, layer, row_offsets, n_cols, *, tm, tn, bias=None, out_dtype=F32, name="matmul"):
    M = xs[0].shape[0]
    tm = min(tm, M)
    assert M % tm == 0 and n_cols % tn == 0
    in_specs = [pl.BlockSpec((tm, x.shape[1]), lambda i, j: (i, 0)) for x in xs]
    for x, off in zip(xs, row_offsets):
        k = x.shape[1]
        assert off % k == 0
        if layer is None:
            in_specs.append(pl.BlockSpec((k, tn), lambda i, j, o=off // k: (o, j)))
        else:
            in_specs.append(pl.BlockSpec((None, k, tn), lambda i, j, o=off // k: (layer, o, j)))
    args = list(xs) + [w] * len(xs)
    if bias is not None:
        in_specs.append(pl.BlockSpec((1, tn), lambda i, j: (0, j)))
        args.append(bias)
    return pl.pallas_call(
        functools.partial(_mm_kernel, n_x=len(xs), has_bias=bias is not None),
        grid=(M // tm, n_cols // tn),
        in_specs=in_specs,
        out_specs=pl.BlockSpec((tm, tn), lambda i, j: (i, j)),
        out_shape=jax.ShapeDtypeStruct((M, n_cols), out_dtype),
        compiler_params=_cparams(("parallel", "arbitrary")),
        name=name,
    )(*args)


def _state_rows_kernel(a_ref, b_ref, o_ref, *, tm, half):
    for j in range(half):
        o_ref[pl.ds(j, tm, stride=2 * half), :] = a_ref[:, j * HEAD_DIM:(j + 1) * HEAD_DIM]
        o_ref[pl.ds(half + j, tm, stride=2 * half), :] = b_ref[:, j * HEAD_DIM:(j + 1) * HEAD_DIM]


def state_rows(proj, col0, width):
    N = proj.shape[0]
    wb = width // 2
    half = wb // HEAD_DIM
    assert col0 % wb == 0
    tm = min(256, N)
    return pl.pallas_call(
        functools.partial(_state_rows_kernel, tm=tm, half=half),
        grid=(N // tm,),
        in_specs=[pl.BlockSpec((tm, wb), lambda i: (i, col0 // wb)),
                  pl.BlockSpec((tm, wb), lambda i: (i, col0 // wb + 1))],
        out_specs=pl.BlockSpec((tm * 2 * half, HEAD_DIM), lambda i: (i, 0)),
        out_shape=jax.ShapeDtypeStruct((N * 2 * half, HEAD_DIM), F32),
        compiler_params=_cparams(("parallel",)),
        name="state_rows",
    )(proj, proj)


def _log_sigmoid(z):
    return jnp.minimum(z, 0.0) - jnp.log1p(jnp.exp(-jnp.abs(z)))


def _sb_prompt_kernel(q_ref, k_ref, v_ref, g_ref, o_ref, *, blk):
    qi = pl.program_id(2)
    scale = HEAD_DIM ** -0.5
    q = q_ref[...].astype(BF16)
    jj = lax.broadcasted_iota(jnp.int32, (blk, blk), 0)
    ss = lax.broadcasted_iota(jnp.int32, (blk, blk), 1)
    later = jnp.where(jj > ss, 1.0, 0.0).astype(BF16)

    def body(i, carry):
        acc, run = carry
        kb = qi - i
        k0 = pl.multiple_of(kb * blk, blk)
        k = k_ref[pl.ds(k0, blk), :].astype(BF16)
        v = v_ref[pl.ds(k0, blk), :].astype(BF16)
        z = lax.dot_general(q, k, (((1,), (1,)), ((), ())), preferred_element_type=F32) * scale
        mask = (kb * blk + ss) < (qi * blk + jj)
        lb = _log_sigmoid(z)
        l1m = jnp.where(mask, lb - z, 0.0)
        hi = l1m.astype(BF16)
        lo = (l1m - hi.astype(F32)).astype(BF16)
        inblk = (jnp.dot(hi, later, preferred_element_type=F32)
                 + jnp.dot(lo, later, preferred_element_type=F32))
        a = jnp.where(mask, jnp.exp(lb + inblk + run), 0.0)
        acc = acc + jnp.dot(a.astype(BF16), v, preferred_element_type=F32)
        run = run + inblk[:, 0:1] + l1m[:, 0:1]
        return acc, run

    acc, _ = lax.fori_loop(0, qi + 1, body,
                           (jnp.zeros((blk, HEAD_DIM), F32), jnp.zeros((blk, 1), F32)))
    ms = jnp.mean(acc * acc, axis=-1, keepdims=True)
    o_ref[...] = (acc * lax.rsqrt(ms + EPS) * g_ref[...]).astype(o_ref.dtype)


def sb_prompt(proj, gain, B, T, *, blk=256):
    blk = min(blk, T)
    nq = T // blk
    return pl.pallas_call(
        functools.partial(_sb_prompt_kernel, blk=blk),
        grid=(B, H_SB, nq),
        in_specs=[pl.BlockSpec((blk, HEAD_DIM), lambda b, h, i: (b * nq + i, h)),
                  pl.BlockSpec((T, HEAD_DIM), lambda b, h, i: (b, H_SB + h)),
                  pl.BlockSpec((T, HEAD_DIM), lambda b, h, i: (b, 2 * H_SB + h)),
                  pl.BlockSpec((1, HEAD_DIM), lambda b, h, i: (0, h))],
        out_specs=pl.BlockSpec((blk, HEAD_DIM), lambda b, h, i: (b * nq + i, h)),
        out_shape=jax.ShapeDtypeStruct((B * T, H_SB * HEAD_DIM), BF16),
        compiler_params=_cparams(("parallel", "parallel", "arbitrary")),
        name="sb_prompt",
    )(proj, proj, proj, gain)


SB_PAGES_PER_STEP = 4
ROWS_PER_TOKEN_SB = 2 * H_SB


def _sb_decode_kernel(pt_ref, qbd_ref, new_ref, tcol_ref, gain_ref, *rest, n_pages_step, page, q_shift):
    page_refs = rest[:n_pages_step]
    o_ref, acc_ref, run_ref = rest[n_pages_step:]
    j = pl.program_id(1)
    scale = HEAD_DIM ** -0.5

    def fold(rows_of, n_keys, masked):
        z = None
        for h in range(H_SB):
            part = jnp.dot(rows_of(0, h).astype(BF16), qbd_ref[0, h * HEAD_DIM:(h + 1) * HEAD_DIM, :].astype(BF16),
                           preferred_element_type=F32)
            z = part if z is None else z + part
        z = z * scale
        lb = _log_sigmoid(z)
        l1m = lb - z
        if masked:
            valid = lax.broadcasted_iota(jnp.int32, (n_keys, LANES), 0) < tcol_ref[...]
            l1m = jnp.where(valid, l1m, 0.0)
        hi = l1m.astype(BF16)
        lo = (l1m - hi.astype(F32)).astype(BF16)
        rs = lax.broadcasted_iota(jnp.int32, (n_keys, n_keys), 0)
        cj = lax.broadcasted_iota(jnp.int32, (n_keys, n_keys), 1)
        later = jnp.where(cj > rs, 1.0, 0.0).astype(BF16)
        inblk = (jnp.dot(later, hi, preferred_element_type=F32) + jnp.dot(later, lo, preferred_element_type=F32))
        a = jnp.exp(lb + inblk + run_ref[...])
        if masked:
            a = jnp.where(valid, a, 0.0)
        run_ref[...] = run_ref[...] + inblk[0:1, :] + l1m[0:1, :]
        at = a.T.astype(BF16)
        for h in range(H_SB):
            acc_ref[h] = acc_ref[h] + jnp.dot(at, rows_of(1, h).astype(BF16), preferred_element_type=F32)

    @pl.when(j == 0)
    def _():
        acc_ref[...] = jnp.zeros(acc_ref.shape, F32)
        run_ref[...] = jnp.zeros(run_ref.shape, F32)
        fold(lambda kv, h: new_ref[0, pl.ds(kv * H_SB + h, LANES, stride=ROWS_PER_TOKEN_SB), :], LANES, True)

    def page_rows(kv, h):
        return jnp.concatenate([r[pl.ds(kv * H_SB + h, page, stride=ROWS_PER_TOKEN_SB), :] for r in page_refs], axis=0)

    fold(page_rows, n_pages_step * page, False)

    @pl.when(j == pl.num_programs(1) - 1)
    def _():
        head_of_row = lax.shift_right_logical(lax.broadcasted_iota(jnp.int32, (LANES, HEAD_DIM), 0), q_shift)
        out = jnp.zeros((LANES, HEAD_DIM), F32)
        for h in range(H_SB):
            out = out + jnp.where(head_of_row == h, acc_ref[h], 0.0)
        ms = jnp.mean(out * out, axis=-1, keepdims=True)
        o_ref[0] = out * lax.rsqrt(ms + EPS) * gain_ref[...]


def sb_decode(q, k_new, v_new, cache, page_table, layer, gain):
    B, Tq, H, hd = q.shape
    depth, n_pool, page = cache.shape[:3]
    n_pages = page_table.shape[1]
    P = SB_PAGES_PER_STEP
    q_shift = Tq.bit_length() - 1
    assert Tq == 1 << q_shift and H * Tq <= LANES and Tq <= LANES and n_pages % P == 0
    eye = jnp.eye(H, dtype=F32)
    qbd = (q.transpose(0, 2, 3, 1)[:, :, :, None, :] * eye[None, :, None, :, None]).reshape(B, H * hd, H * Tq)
    qbd = jnp.pad(qbd, ((0, 0), (0, 0), (0, LANES - H * Tq)))
    new = jnp.stack([k_new, v_new], axis=2).reshape(B, Tq * ROWS_PER_TOKEN_SB, hd)
    new = jnp.pad(new, ((0, 0), (0, (LANES - Tq) * ROWS_PER_TOKEN_SB), (0, 0)))
    lane = jnp.arange(LANES, dtype=jnp.int32)
    tcol = jnp.where(lane < H * Tq, lane % Tq, -1)[None, :]
    gain_rows = jnp.pad(jnp.repeat(gain.reshape(H, hd), Tq, axis=0), ((0, LANES - H * Tq), (0, 0)),
                        constant_values=1.0)
    pages = cache.reshape(depth, n_pool, page * ROWS_PER_TOKEN_SB, hd)
    page_spec = lambda r: pl.BlockSpec(
        (None, None, page * ROWS_PER_TOKEN_SB, hd),
        lambda b, j, pt: (layer, pt[b, n_pages - (j + 1) * P + r], 0, 0))
    out = pl.pallas_call(
        functools.partial(_sb_decode_kernel, n_pages_step=P, page=page, q_shift=q_shift),
        grid_spec=pltpu.PrefetchScalarGridSpec(
            num_scalar_prefetch=1, grid=(B, n_pages // P),
            in_specs=[pl.BlockSpec((1, H * hd, LANES), lambda b, j, pt: (b, 0, 0)),
                      pl.BlockSpec((1, LANES * ROWS_PER_TOKEN_SB, hd), lambda b, j, pt: (b, 0, 0)),
                      pl.BlockSpec((1, LANES), lambda b, j, pt: (0, 0)),
                      pl.BlockSpec((LANES, hd), lambda b, j, pt: (0, 0))] + [page_spec(r) for r in range(P)],
            out_specs=pl.BlockSpec((1, LANES, hd), lambda b, j, pt: (b, 0, 0)),
            scratch_shapes=[pltpu.VMEM((H, LANES, hd), F32), pltpu.VMEM((1, LANES), F32)]),
        out_shape=jax.ShapeDtypeStruct((B, LANES, hd), F32),
        compiler_params=_cparams(("parallel", "arbitrary")),
        name="sb_decode",
    )(page_table, qbd, new, tcol, gain_rows, *([pages] * P))
    return out[:, :H * Tq].reshape(B, H, Tq, hd).transpose(0, 2, 1, 3).reshape(B, Tq, H * hd)


HIST = 32


def _conv_kernel(a_ref, b_ref, hist_ref, w_ref, cb_ref, g_ref, beta_ref, y_ref, st_ref, ext_ref, *, tb, t_valid):
    i = pl.program_id(1)

    @pl.when(i == 0)
    def _():
        ext_ref[0:HIST, :] = hist_ref[0]

    u = a_ref[...] * jax.nn.sigmoid(b_ref[...])
    ext_ref[HIST:HIST + tb, :] = u
    acc = jnp.zeros(u.shape, F32)
    for k in range(CONV_W):
        acc = acc + ext_ref[pl.ds(HIST - (CONV_W - 1) + k, tb), :] * w_ref[k:k + 1, :]
    y = acc + cb_ref[...]
    mu = jnp.mean(y, axis=-1, keepdims=True)
    var = jnp.mean(jnp.square(y - mu), axis=-1, keepdims=True)
    yn = (y - mu) * lax.rsqrt(var + EPS) * g_ref[...] + beta_ref[...]
    y_ref[...] = (yn * jax.nn.sigmoid(yn)).astype(y_ref.dtype)
    st_ref[0] = ext_ref[pl.ds(HIST + t_valid - (CONV_W - 1), CONV_W - 1), :]
    if tb >= HIST:
        ext_ref[0:HIST, :] = ext_ref[tb:tb + HIST, :]


def conformer_conv(proj, hist, conv_w, conv_b, ln_g, ln_b, B, T, t_valid, *, tb=256):
    C = conv_w.shape[-1]
    tb = min(tb, T)
    nt = T // tb
    assert nt == 1 or (tb >= HIST and t_valid == tb)
    return pl.pallas_call(
        functools.partial(_conv_kernel, tb=tb, t_valid=t_valid),
        grid=(B, nt),
        in_specs=[pl.BlockSpec((tb, C), lambda b, i: (b * nt + i, 3)),
                  pl.BlockSpec((tb, C), lambda b, i: (b * nt + i, 4)),
                  pl.BlockSpec((1, HIST, C), lambda b, i: (b, 0, 0)),
                  pl.BlockSpec((CONV_W, C), lambda b, i: (0, 0)),
                  pl.BlockSpec((1, C), lambda b, i: (0, 0)),
                  pl.BlockSpec((1, C), lambda b, i: (0, 0)),
                  pl.BlockSpec((1, C), lambda b, i: (0, 0))],
        out_specs=[pl.BlockSpec((tb, C), lambda b, i: (b * nt + i, 0)),
                   pl.BlockSpec((1, CONV_W - 1, C), lambda b, i: (b, 0, 0))],
        out_shape=[jax.ShapeDtypeStruct((B * T, C), BF16),
                   jax.ShapeDtypeStruct((B, CONV_W - 1, C), F32)],
        scratch_shapes=[pltpu.VMEM((HIST + tb, C), F32)],
        compiler_params=_cparams(("parallel", "arbitrary")),
        name="conformer_conv",
    )(proj, proj, hist, conv_w, conv_b, ln_g, ln_b)


def _compress_kernel(x_ref, w_ref, pe_ref, o_ref, *, n_chunks):
    parts = []
    for r in range(L_CMP // CMP_STRIDE):
        p = jnp.zeros((n_chunks, HEAD_DIM), F32)
        for l in range(CMP_STRIDE):
            j = r * CMP_STRIDE + l
            rows = x_ref[pl.ds(l, n_chunks, stride=CMP_STRIDE), :] + pe_ref[j:j + 1, :]
            p = p + jnp.dot(rows.astype(BF16), w_ref[j].astype(BF16), preferred_element_type=F32)
        parts.append(p)
    o_ref[...] = parts[0] + pltpu.roll(parts[1], n_chunks - 1, 0)


def compress_kv(rows, col_block0, cmp_w, cmp_pe, layer, B, T):
    n_chunks = T // CMP_STRIDE
    return pl.pallas_call(
        functools.partial(_compress_kernel, n_chunks=n_chunks),
        grid=(B, 2, H_KV),
        in_specs=[pl.BlockSpec((T, HEAD_DIM), lambda b, ty, h: (b, col_block0 + ty * H_KV + h)),
                  pl.BlockSpec((None, None, L_CMP, HEAD_DIM, HEAD_DIM), lambda b, ty, h: (layer, ty, 0, 0, 0)),
                  pl.BlockSpec((None, None, L_CMP, HEAD_DIM), lambda b, ty, h: (layer, ty, 0, 0))],
        out_specs=pl.BlockSpec((None, None, None, n_chunks, HEAD_DIM), lambda b, ty, h: (b, ty, h, 0, 0)),
        out_shape=jax.ShapeDtypeStruct((B, 2, H_KV, n_chunks, HEAD_DIM), F32),
        compiler_params=_cparams(("parallel", "parallel", "parallel")),
        name="nsa_compress",
    )(rows, cmp_w, cmp_pe)


def _nsa_prompt_kernel(q_ref, kc_ref, vc_ref, sk_ref, sv_ref, wk_ref, wv_ref, gl_ref, bg_ref, gain_ref,
                       o_ref, *, tq, bk, n_sel):
    qi = pl.program_id(2)
    scale = HEAD_DIM ** -0.5
    rows = G_NSA * tq
    q0 = qi * tq
    q = jnp.concatenate([q_ref[:, g * HEAD_DIM:(g + 1) * HEAD_DIM] for g in range(G_NSA)], axis=0).astype(BF16)
    nt = (((1,), (1,)), ((), ()))

    n_cmp_pad = kc_ref.shape[0]
    s = lax.dot_general(q, kc_ref[...].astype(BF16), nt, preferred_element_type=F32) * scale
    cl = lax.broadcasted_iota(jnp.int32, (rows, n_cmp_pad), 1)
    qp = q0 + (lax.broadcasted_iota(jnp.int32, (rows, n_cmp_pad), 0) & (tq - 1))
    cmask = (cl * CMP_STRIDE + (L_CMP - 1)) <= qp
    sm = jnp.where(cmask, s, NEG)
    e = jnp.where(cmask, jnp.exp(sm - jnp.max(sm, axis=-1, keepdims=True)), 0.0)
    den = jnp.sum(e, axis=-1, keepdims=True)
    p = e / jnp.where(den > 0.0, den, 1.0)
    pb = p.astype(BF16)
    o_cmp = jnp.dot(pb, vc_ref[...].astype(BF16), preferred_element_type=F32)

    cn = lax.broadcasted_iota(jnp.int32, (n_cmp_pad, LANES), 0) * CMP_STRIDE
    sj = lax.broadcasted_iota(jnp.int32, (n_cmp_pad, LANES), 1) * L_SEL
    ov = jnp.clip(jnp.minimum(cn + L_CMP, sj + L_SEL) - jnp.maximum(cn, sj), 0, None).astype(F32) / L_CMP
    ov = ov.astype(BF16)
    imp = None
    for g in range(G_NSA):
        part = jnp.dot(pb[g * tq:(g + 1) * tq], ov, preferred_element_type=F32)
        imp = part if imp is None else imp + part
    blk = lax.broadcasted_iota(jnp.int32, (tq, LANES), 1)
    qpt = q0 + lax.broadcasted_iota(jnp.int32, (tq, LANES), 0)
    cur = jnp.right_shift(qpt, SEL_SHIFT)
    forced = (blk == 0) | (blk == cur) | (blk == cur - 1)
    valid = blk * L_SEL <= qpt
    score = jnp.where(forced, BIG, jnp.where(valid, imp, -BIG))
    score = jnp.where(blk < n_sel, score, -3e38)
    rank = jnp.zeros((tq, LANES), jnp.int32)
    for i in range(n_sel):
        ci = score[:, i:i + 1]
        ahead = (ci > score) | ((ci == score) & (blk > i))
        rank = rank + ahead.astype(jnp.int32)
    sel = jnp.where((rank < min(N_SEL, n_sel)) & (blk < n_sel), 1.0, 0.0).astype(BF16)

    kj = lax.broadcasted_iota(jnp.int32, (LANES, bk), 0)
    ks = lax.broadcasted_iota(jnp.int32, (LANES, bk), 1)
    klane = lax.broadcasted_iota(jnp.int32, (rows, bk), 1)
    qpos = q0 + (lax.broadcasted_iota(jnp.int32, (rows, bk), 0) & (tq - 1))

    def flash_step(k_ref, v_ref, kb, allowed_fn, carry):
        m, l, acc = carry
        k0 = pl.multiple_of(kb * bk, bk)
        k = k_ref[pl.ds(k0, bk), :].astype(BF16)
        v = v_ref[pl.ds(k0, bk), :].astype(BF16)
        sc = lax.dot_general(q, k, nt, preferred_element_type=F32) * scale
        allowed = allowed_fn(kb * bk + klane)
        scm = jnp.where(allowed, sc, NEG)
        m_new = jnp.maximum(m, jnp.max(scm, axis=-1, keepdims=True))
        alpha = jnp.exp(m - m_new)
        pe = jnp.where(allowed, jnp.exp(scm - m_new), 0.0)
        l = alpha * l + jnp.sum(pe, axis=-1, keepdims=True)
        acc = alpha * acc + jnp.dot(pe.astype(BF16), v, preferred_element_type=F32)
        return m_new, l, acc

    init = (jnp.full((rows, 1), NEG, F32), jnp.zeros((rows, 1), F32), jnp.zeros((rows, HEAD_DIM), F32))

    def slc_body(kb, carry):
        expand = jnp.where(jnp.right_shift(kb * bk + ks, SEL_SHIFT) == kj, 1.0, 0.0).astype(BF16)
        selk = jnp.dot(sel, expand, preferred_element_type=F32)
        selk = jnp.concatenate([selk] * G_NSA, axis=0)
        return flash_step(sk_ref, sv_ref, kb, lambda kpos: (selk > 0.5) & (kpos <= qpos), carry)

    last_kb = (q0 + tq - 1) // bk
    _, l_s, acc_s = lax.fori_loop(0, last_kb + 1, slc_body, init)
    o_slc = acc_s / l_s

    def win_body(kb, carry):
        return flash_step(wk_ref, wv_ref, kb,
                          lambda kpos: (qpos - kpos >= 0) & (qpos - kpos < WINDOW), carry)

    first_kb = jnp.maximum(q0 - (WINDOW - 1), 0) // bk
    _, l_w, acc_w = lax.fori_loop(first_kb, last_kb + 1, win_body, init)
    o_win = acc_w / l_w

    gates = jax.nn.sigmoid(gl_ref[...] + bg_ref[...])
    for g in range(G_NSA):
        r = slice(g * tq, (g + 1) * tq)
        o = (gates[:, g:g + 1] * o_cmp[r] + gates[:, G_NSA + g:G_NSA + g + 1] * o_slc[r]
             + gates[:, 2 * G_NSA + g:2 * G_NSA + g + 1] * o_win[r])
        ms = jnp.mean(o * o, axis=-1, keepdims=True)
        o = o * lax.rsqrt(ms + EPS) * gain_ref[:, g * HEAD_DIM:(g + 1) * HEAD_DIM]
        o_ref[:, g * HEAD_DIM:(g + 1) * HEAD_DIM] = o.astype(o_ref.dtype)


def nsa_prompt(proj, kvc, gate_logits, b_gate, gain, B, T):
    tq = LANES
    bk = min(2 * LANES, T)
    assert T % tq == 0 and T % bk == 0 and bk % L_SEL == 0
    nq = T // tq
    n_sel = -(-T // L_SEL)
    assert n_sel <= LANES
    n_cmp_pad = kvc.shape[3]
    qw = G_NSA * HEAD_DIM
    seq = lambda c0: pl.BlockSpec((T, HEAD_DIM), lambda b, h, i: (b, c0 + h))
    return pl.pallas_call(
        functools.partial(_nsa_prompt_kernel, tq=tq, bk=bk, n_sel=n_sel),
        grid=(B, H_KV, nq),
        in_specs=[pl.BlockSpec((tq, qw), lambda b, h, i: (b * nq + i, 5120 // qw + h)),
                  pl.BlockSpec((None, None, None, n_cmp_pad, HEAD_DIM), lambda b, h, i: (b, 0, h, 0, 0)),
                  pl.BlockSpec((None, None, None, n_cmp_pad, HEAD_DIM), lambda b, h, i: (b, 1, h, 0, 0)),
                  seq(64), seq(68), seq(72), seq(76),
                  pl.BlockSpec((tq, LANES), lambda b, h, i: (b * nq + i, h)),
                  pl.BlockSpec((1, LANES), lambda b, h, i: (0, h)),
                  pl.BlockSpec((1, qw), lambda b, h, i: (0, h))],
        out_specs=pl.BlockSpec((tq, qw), lambda b, h, i: (b * nq + i, h)),
        out_shape=jax.ShapeDtypeStruct((B * T, H_KV * qw), BF16),
        compiler_params=_cparams(("parallel", "parallel", "arbitrary")),
        name="nsa_prompt",
    )(proj, kvc, kvc, proj, proj, proj, proj, gate_logits, b_gate, gain)


def _layer_norm(y, g, b):
    mu = jnp.mean(y, axis=-1, keepdims=True)
    var = jnp.mean(jnp.square(y - mu), axis=-1, keepdims=True)
    return (y - mu) * lax.rsqrt(var + EPS) * g + b


def _ln1_kernel(x_ref, h_ref, g1_ref, lg_ref, lb_ref, sc_ref, sh_ref, wr_ref, br_ref, x1_ref, x2_ref, rl_ref, *, alpha):
    x1 = _layer_norm(alpha * x_ref[0] + g1_ref[0] * h_ref[0], lg_ref[...], lb_ref[...])
    x1_ref[0] = x1
    x2 = x1 * (1.0 + sc_ref[0]) + sh_ref[0]
    x2_ref[0] = x2.astype(x2_ref.dtype)
    rl_ref[0] = jnp.dot(x2.astype(BF16), wr_ref[...].astype(BF16), preferred_element_type=F32) + br_ref[...]


def post_mixer(x, h, g1, ln_g, ln_b, sc2, sh2, w_router, b_router, alpha):
    B, T, D = x.shape
    tt = min(T, 256)
    row = pl.BlockSpec((1, tt, D), lambda b, i: (b, i, 0))
    per_b = pl.BlockSpec((1, 1, D), lambda b, i: (b, 0, 0))
    vec = pl.BlockSpec((1, D), lambda b, i: (0, 0))
    return pl.pallas_call(
        functools.partial(_ln1_kernel, alpha=alpha),
        grid=(B, T // tt),
        in_specs=[row, row, per_b, vec, vec, per_b, per_b,
                  pl.BlockSpec((D, LANES), lambda b, i: (0, 0)),
                  pl.BlockSpec((1, LANES), lambda b, i: (0, 0))],
        out_specs=[row, row, pl.BlockSpec((1, tt, LANES), lambda b, i: (b, i, 0))],
        out_shape=[jax.ShapeDtypeStruct((B, T, D), F32), jax.ShapeDtypeStruct((B, T, D), BF16),
                   jax.ShapeDtypeStruct((B, T, LANES), F32)],
        compiler_params=_cparams(("parallel", "parallel")),
        name="post_mixer_ln",
    )(x, h, g1, ln_g, ln_b, sc2, sh2, w_router, b_router)


def _ln2_kernel(x_ref, f_ref, g2_ref, lg_ref, lb_ref, o_ref, *, alpha):
    o_ref[0] = _layer_norm(alpha * x_ref[0] + g2_ref[0] * f_ref[0], lg_ref[...], lb_ref[...])


def post_ffn(x, f, g2, ln_g, ln_b, alpha):
    B, T, D = x.shape
    tt = min(T, 256)
    row = pl.BlockSpec((1, tt, D), lambda b, i: (b, i, 0))
    return pl.pallas_call(
        functools.partial(_ln2_kernel, alpha=alpha),
        grid=(B, T // tt),
        in_specs=[row, row, pl.BlockSpec((1, 1, D), lambda b, i: (b, 0, 0)),
                  pl.BlockSpec((1, D), lambda b, i: (0, 0)), pl.BlockSpec((1, D), lambda b, i: (0, 0))],
        out_specs=row,
        out_shape=jax.ShapeDtypeStruct((B, T, D), F32),
        compiler_params=_cparams(("parallel", "parallel")),
        name="post_ffn_ln",
    )(x, f, g2, ln_g, ln_b)


def _moe_up_kernel(x_ref, wg_ref, wu_ref, cw_ref, h_ref):
    x = x_ref[...]
    hg = jnp.dot(x, wg_ref[...].astype(BF16), preferred_element_type=F32)
    hu = jnp.dot(x, wu_ref[...].astype(BF16), preferred_element_type=F32)
    h_ref[...] = (hg * jax.nn.sigmoid(hg) * hu * cw_ref[...]).astype(h_ref.dtype)


def _mm_acc_kernel(x_ref, w_ref, o_ref, acc_ref):
    k = pl.program_id(2)

    @pl.when(k == 0)
    def _():
        acc_ref[...] = jnp.zeros(acc_ref.shape, F32)

    acc_ref[...] += jnp.dot(x_ref[...].astype(BF16), w_ref[...].astype(BF16), preferred_element_type=F32)

    @pl.when(k == pl.num_programs(2) - 1)
    def _():
        o_ref[...] = acc_ref[...]


def hier_moe(x2, rl, w_gate, w_up, w_down, layer, *, tm):
    N, D = x2.shape
    E = N_GROUPS * E_PER_GROUP
    F = D_EXPERT
    g_prob = jax.nn.softmax(rl[:, :N_GROUPS], axis=-1)
    g_p, g_idx = lax.top_k(g_prob, 1)
    e_logits = rl[:, N_GROUPS:N_GROUPS + E].reshape(N, N_GROUPS, E_PER_GROUP)
    e_logits = jnp.take_along_axis(e_logits, g_idx[:, :, None], axis=1)[:, 0]
    e_p, e_idx = lax.top_k(jax.nn.softmax(e_logits, axis=-1), 2)
    w = g_p * e_p / jnp.sum(e_p, -1, keepdims=True)
    expert = g_idx * E_PER_GROUP + e_idx
    combine = jnp.sum(jax.nn.one_hot(expert, E, dtype=F32) * w[..., None], axis=1)
    tm = min(tm, N)
    fc = F // 2
    per = F // fc
    w_spec = pl.BlockSpec((None, None, D, fc), lambda i, c: (layer, c // per, 0, c % per))
    h = pl.pallas_call(
        _moe_up_kernel,
        grid=(N // tm, E * per),
        in_specs=[pl.BlockSpec((tm, D), lambda i, c: (i, 0)), w_spec, w_spec,
                  pl.BlockSpec((None, tm, 1), lambda i, c: (c // per, i, 0))],
        out_specs=pl.BlockSpec((tm, fc), lambda i, c: (i, c)),
        out_shape=jax.ShapeDtypeStruct((N, E * F), BF16),
        compiler_params=_cparams(("parallel", "arbitrary")),
        name="moe_gate_up",
    )(x2, w_gate, w_up, combine.T[:, :, None])
    wd = w_down.reshape(w_down.shape[0], E * F, D)
    tn = tk = 1024
    return pl.pallas_call(
        _mm_acc_kernel,
        grid=(N // tm, D // tn, E * F // tk),
        in_specs=[pl.BlockSpec((tm, tk), lambda i, j, k: (i, k)),
                  pl.BlockSpec((None, tk, tn), lambda i, j, k: (layer, k, j))],
        out_specs=pl.BlockSpec((tm, tn), lambda i, j, k: (i, j)),
        out_shape=jax.ShapeDtypeStruct((N, D), F32),
        scratch_shapes=[pltpu.VMEM((tm, tn), F32)],
        compiler_params=_cparams(("parallel", "parallel", "arbitrary")),
        name="moe_down",
    )(h, wd)


ROWS_PER_TOKEN_NSA = 4 * H_KV
CMP_PAGES_PER_STEP = 16
NSA_PAGES_PER_STEP = 4
Q_ROWS = 8


def _compress_paged_kernel(pt_ref, w_ref, pe_ref, *rest, n_pages_step, page):
    page_refs = rest[:n_pages_step]
    o_ref = rest[n_pages_step]
    cpp = page // CMP_STRIDE
    m = n_pages_step * cpp
    for ty in range(2):
        for r in range(L_CMP // CMP_STRIDE):
            acc = jnp.zeros((H_KV * m, HEAD_DIM), F32)
            for l in range(CMP_STRIDE):
                j = r * CMP_STRIDE + l
                rows = jnp.concatenate(
                    [p[pl.ds(l * ROWS_PER_TOKEN_NSA + ty * H_KV + h, cpp, stride=CMP_STRIDE * ROWS_PER_TOKEN_NSA), :]
                     for h in range(H_KV) for p in page_refs], axis=0)
                rows = (rows + pe_ref[ty, j:j + 1, :]).astype(BF16)
                acc = acc + jnp.dot(rows, w_ref[ty, j].astype(BF16), preferred_element_type=F32)
            for h in range(H_KV):
                o_ref[r, ty, h] = acc[h * m:(h + 1) * m]


def compress_paged(pages, page_table, cmp_w, cmp_pe, layer):
    depth, n_pool, prow, hd = pages.shape
    page = prow // ROWS_PER_TOKEN_NSA
    B, n_pages = page_table.shape
    P = CMP_PAGES_PER_STEP
    assert n_pages % P == 0 and page % CMP_STRIDE == 0
    cpp = page // CMP_STRIDE
    page_spec = lambda r: pl.BlockSpec((None, None, prow, hd), lambda b, j, pt: (layer, pt[b, j * P + r], 0, 0))
    return pl.pallas_call(
        functools.partial(_compress_paged_kernel, n_pages_step=P, page=page),
        grid_spec=pltpu.PrefetchScalarGridSpec(
            num_scalar_prefetch=1, grid=(B, n_pages // P),
            in_specs=[pl.BlockSpec((None, 2, L_CMP, hd, hd), lambda b, j, pt: (layer, 0, 0, 0, 0)),
                      pl.BlockSpec((None, 2, L_CMP, hd), lambda b, j, pt: (layer, 0, 0, 0))]
            + [page_spec(r) for r in range(P)],
            out_specs=pl.BlockSpec((None, 2, 2, H_KV, P * cpp, hd), lambda b, j, pt: (b, 0, 0, 0, j, 0))),
        out_shape=jax.ShapeDtypeStruct((B, 2, 2, H_KV, n_pages * cpp, hd), F32),
        compiler_params=_cparams(("parallel", "arbitrary")),
        name="nsa_compress_paged",
    )(page_table, cmp_w, cmp_pe, *([pages] * P))


def _nsa_decode_front_kernel(q_ref, p_ref, wk_ref, wv_ref, ocmp_ref, owin_ref, sel_ref, *,
                             past_len, n_win, n_sel, sel_w):
    scale = HEAD_DIM ** -0.5
    rows = G_NSA * Q_ROWS
    nt = (((1,), (1,)), ((), ()))
    q = q_ref[...].astype(BF16)
    n_cmp = p_ref.shape[2]
    kc = p_ref[0, 0] + pltpu.roll(p_ref[1, 0], n_cmp - 1, 0)
    vc = p_ref[0, 1] + pltpu.roll(p_ref[1, 1], n_cmp - 1, 0)
    s = lax.dot_general(q, kc.astype(BF16), nt, preferred_element_type=F32) * scale
    cl = lax.broadcasted_iota(jnp.int32, (rows, n_cmp), 1)
    qp = past_len + (lax.broadcasted_iota(jnp.int32, (rows, n_cmp), 0) & (Q_ROWS - 1))
    cmask = (cl * CMP_STRIDE + (L_CMP - 1)) <= qp
    sm = jnp.where(cmask, s, NEG)
    e = jnp.where(cmask, jnp.exp(sm - jnp.max(sm, axis=-1, keepdims=True)), 0.0)
    den = jnp.sum(e, axis=-1, keepdims=True)
    pb = (e / jnp.where(den > 0.0, den, 1.0)).astype(BF16)
    ocmp_ref[...] = jnp.dot(pb, vc.astype(BF16), preferred_element_type=F32)
    cn = lax.broadcasted_iota(jnp.int32, (n_cmp, sel_w), 0) * CMP_STRIDE
    sj = lax.broadcasted_iota(jnp.int32, (n_cmp, sel_w), 1) * L_SEL
    ov = (jnp.clip(jnp.minimum(cn + L_CMP, sj + L_SEL) - jnp.maximum(cn, sj), 0, None).astype(F32) / L_CMP).astype(BF16)
    imp = None
    for g in range(G_NSA):
        part = jnp.dot(pb[g * Q_ROWS:(g + 1) * Q_ROWS], ov, preferred_element_type=F32)
        imp = part if imp is None else imp + part
    blk = lax.broadcasted_iota(jnp.int32, (Q_ROWS, sel_w), 1)
    qpt = past_len + lax.broadcasted_iota(jnp.int32, (Q_ROWS, sel_w), 0)
    cur = jnp.right_shift(qpt, SEL_SHIFT)
    forced = (blk == 0) | (blk == cur) | (blk == cur - 1)
    valid = blk * L_SEL <= qpt
    score = jnp.where(forced, BIG, jnp.where(valid, imp, -BIG))
    score = jnp.where(blk < n_sel, score, -3e38)
    rank = jnp.zeros((Q_ROWS, sel_w), jnp.int32)
    for i in range(n_sel):
        ci = score[:, i:i + 1]
        ahead = (ci > score) | ((ci == score) & (blk > i))
        rank = rank + ahead.astype(jnp.int32)
    sel_ref[...] = jnp.where((rank < min(N_SEL, n_sel)) & (blk < n_sel), 1.0, 0.0)
    n_wpad = wk_ref.shape[0]
    sw = lax.dot_general(q, wk_ref[...].astype(BF16), nt, preferred_element_type=F32) * scale
    wi = lax.broadcasted_iota(jnp.int32, (rows, n_wpad), 1)
    wq = past_len + (lax.broadcasted_iota(jnp.int32, (rows, n_wpad), 0) & (Q_ROWS - 1))
    wpos = wi + (past_len - n_win)
    d = wq - wpos
    wmask = (d >= 0) & (d < WINDOW) & (wpos >= 0)
    swm = jnp.where(wmask, sw, NEG)
    ew = jnp.where(wmask, jnp.exp(swm - jnp.max(swm, axis=-1, keepdims=True)), 0.0)
    pw = ew / jnp.sum(ew, axis=-1, keepdims=True)
    owin_ref[...] = jnp.dot(pw.astype(BF16), wv_ref[...].astype(BF16), preferred_element_type=F32)


def _nsa_decode_sweep_kernel(pt_ref, q_ref, sel_ref, new_ref, ocmp_ref, owin_ref, gl_ref, gb_ref, gain_ref, *rest,
                             n_pages_step, page, past_len, sel_w):
    page_refs = rest[:n_pages_step]
    o_ref, m_ref, l_ref, acc_ref = rest[n_pages_step:]
    j = pl.program_id(1)
    scale = HEAD_DIM ** -0.5
    rows = G_NSA * Q_ROWS
    nt = (((1,), (1,)), ((), ()))

    def fold(rows_of, n_keys, pos0):
        kpos = pos0 + lax.broadcasted_iota(jnp.int32, (rows, n_keys), 1)
        qpos = past_len + (lax.broadcasted_iota(jnp.int32, (rows, n_keys), 0) & (Q_ROWS - 1))
        eb = lax.broadcasted_iota(jnp.int32, (sel_w, n_keys), 0)
        ek = pos0 + lax.broadcasted_iota(jnp.int32, (sel_w, n_keys), 1)
        expand = jnp.where(jnp.right_shift(ek, SEL_SHIFT) == eb, 1.0, 0.0).astype(BF16)
        for h in range(H_KV):
            r = slice(h * rows, (h + 1) * rows)
            k = rows_of(2 * H_KV + h).astype(BF16)
            v = rows_of(3 * H_KV + h).astype(BF16)
            s = lax.dot_general(q_ref[r, :].astype(BF16), k, nt, preferred_element_type=F32) * scale
            selk = jnp.dot(sel_ref[h].astype(BF16), expand, preferred_element_type=F32)
            selk = jnp.concatenate([selk] * G_NSA, axis=0)
            allowed = (selk > 0.5) & (kpos <= qpos)
            sm = jnp.where(allowed, s, NEG)
            m_old = m_ref[r, :]
            m_new = jnp.maximum(m_old, jnp.max(sm, axis=-1, keepdims=True))
            alpha = jnp.exp(m_old - m_new)
            pe = jnp.where(allowed, jnp.exp(sm - m_new), 0.0)
            l_ref[r, :] = alpha * l_ref[r, :] + jnp.sum(pe, axis=-1, keepdims=True)
            acc_ref[r, :] = alpha * acc_ref[r, :] + jnp.dot(pe.astype(BF16), v, preferred_element_type=F32)
            m_ref[r, :] = m_new

    @pl.when(j == 0)
    def _():
        m_ref[...] = jnp.full(m_ref.shape, NEG, F32)
        l_ref[...] = jnp.zeros(l_ref.shape, F32)
        acc_ref[...] = jnp.zeros(acc_ref.shape, F32)
        fold(lambda c: new_ref[pl.ds(c, LANES, stride=ROWS_PER_TOKEN_NSA), :], LANES, past_len)

    def page_rows(c):
        return jnp.concatenate([p[pl.ds(c, page, stride=ROWS_PER_TOKEN_NSA), :] for p in page_refs], axis=0)

    fold(page_rows, n_pages_step * page, j * (n_pages_step * page))

    @pl.when(j == pl.num_programs(1) - 1)
    def _():
        gates = jax.nn.sigmoid(gl_ref[...] + gb_ref[...])
        o = (gates[:, 0:1] * ocmp_ref[...] + gates[:, 1:2] * (acc_ref[...] / l_ref[...])
             + gates[:, 2:3] * owin_ref[...])
        ms = jnp.mean(o * o, axis=-1, keepdims=True)
        o_ref[...] = o * lax.rsqrt(ms + EPS) * gain_ref[...]


def nsa_decode(q_n, nsa_new, win_all, gate_logits, b_gate, gain, pages, page_table, cmp_w, cmp_pe, layer, past_len):
    B, T, H, G, hd = q_n.shape
    assert T <= Q_ROWS and H == H_KV and G == G_NSA
    rows = G * Q_ROWS
    page = pages.shape[2] // ROWS_PER_TOKEN_NSA
    n_pages = page_table.shape[1]
    n_sel = -(-(past_len + T) // L_SEL)
    sel_w = -(-n_sel // LANES) * LANES
    n_win = win_all.shape[1] - T
    n_wpad = -(-(n_win + Q_ROWS) // LANES) * LANES
    kvc = compress_paged(pages, page_table, cmp_w, cmp_pe, layer)
    n_cmp = kvc.shape[4]
    tpad = lambda a, axis: jnp.pad(a, [(0, Q_ROWS - T) if i == axis else (0, 0) for i in range(a.ndim)])
    q_rows = tpad(q_n.transpose(0, 2, 3, 1, 4), 3).reshape(B, H, rows, hd)
    win_t = win_all.transpose(0, 2, 3, 1, 4)
    win_t = jnp.pad(win_t, ((0, 0), (0, 0), (0, 0), (0, n_wpad - n_win - T), (0, 0)))
    per_bh = lambda *shape: pl.BlockSpec((None, None) + shape, lambda b, h: (b, h) + (0,) * len(shape))
    o_cmp, o_win, sel = pl.pallas_call(
        functools.partial(_nsa_decode_front_kernel, past_len=past_len, n_win=n_win, n_sel=n_sel, sel_w=sel_w),
        grid=(B, H),
        in_specs=[per_bh(rows, hd),
                  pl.BlockSpec((None, 2, 2, None, n_cmp, hd), lambda b, h: (b, 0, 0, h, 0, 0)),
                  pl.BlockSpec((None, None, None, n_wpad, hd), lambda b, h: (b, 0, h, 0, 0)),
                  pl.BlockSpec((None, None, None, n_wpad, hd), lambda b, h: (b, 1, h, 0, 0))],
        out_specs=[per_bh(rows, hd), per_bh(rows, hd), per_bh(Q_ROWS, sel_w)],
        out_shape=[jax.ShapeDtypeStruct((B, H, rows, hd), F32), jax.ShapeDtypeStruct((B, H, rows, hd), F32),
                   jax.ShapeDtypeStruct((B, H, Q_ROWS, sel_w), F32)],
        compiler_params=_cparams(("parallel", "parallel")),
        name="nsa_decode_front",
    )(q_rows, kvc, win_t, win_t)
    gl = gate_logits.reshape(B, T, 3, H, G).transpose(0, 3, 4, 1, 2)
    gl = jnp.pad(tpad(gl, 3), ((0, 0),) * 4 + ((0, LANES - 3),)).reshape(B, H * rows, LANES)
    gb = jnp.broadcast_to(b_gate.reshape(3, H, G).transpose(1, 2, 0)[:, :, None, :], (H, G, Q_ROWS, 3))
    gb = jnp.pad(gb, ((0, 0),) * 3 + ((0, LANES - 3),)).reshape(H * rows, LANES)
    gain_rows = jnp.broadcast_to(gain.reshape(H, G, 1, hd), (H, G, Q_ROWS, hd)).reshape(H * rows, hd)
    new = nsa_new.reshape(B, T * ROWS_PER_TOKEN_NSA, hd)
    new = jnp.pad(new, ((0, 0), (0, (LANES - T) * ROWS_PER_TOKEN_NSA), (0, 0)))
    P = NSA_PAGES_PER_STEP
    assert n_pages % P == 0
    per_b = lambda *shape: pl.BlockSpec((None,) + shape, lambda b, j, pt: (b,) + (0,) * len(shape))
    const = lambda *shape: pl.BlockSpec(shape, lambda b, j, pt: (0,) * len(shape))
    page_spec = lambda r: pl.BlockSpec((None, None, page * ROWS_PER_TOKEN_NSA, hd),
                                       lambda b, j, pt: (layer, pt[b, j * P + r], 0, 0))
    out = pl.pallas_call(
        functools.partial(_nsa_decode_sweep_kernel, n_pages_step=P, page=page, past_len=past_len, sel_w=sel_w),
        grid_spec=pltpu.PrefetchScalarGridSpec(
            num_scalar_prefetch=1, grid=(B, n_pages // P),
            in_specs=[per_b(H * rows, hd), per_b(H, Q_ROWS, sel_w), per_b(LANES * ROWS_PER_TOKEN_NSA, hd),
                      per_b(H * rows, hd), per_b(H * rows, hd), per_b(H * rows, LANES),
                      const(H * rows, LANES), const(H * rows, hd)] + [page_spec(r) for r in range(P)],
            out_specs=per_b(H * rows, hd),
            scratch_shapes=[pltpu.VMEM((H * rows, 1), F32), pltpu.VMEM((H * rows, 1), F32),
                            pltpu.VMEM((H * rows, hd), F32)]),
        out_shape=jax.ShapeDtypeStruct((B, H * rows, hd), F32),
        compiler_params=_cparams(("parallel", "arbitrary")),
        name="nsa_decode_sweep",
    )(page_table, q_rows.reshape(B, H * rows, hd), sel, new, o_cmp.reshape(B, H * rows, hd),
      o_win.reshape(B, H * rows, hd), gl, gb, gain_rows, *([pages] * P))
    out = out.reshape(B, H, G, Q_ROWS, hd)[:, :, :, :T]
    return out.transpose(0, 3, 1, 2, 4).reshape(B, T, H * G * hd)


def _gate_weight(w_in, layer, n_main):
    D = w_in.shape[1]
    wg = w_in[layer, :, n_main:n_main + 3 * H_KV * G_NSA].reshape(D, 3, H_KV, G_NSA)
    wg = wg.transpose(0, 2, 1, 3).reshape(D, H_KV, 3 * G_NSA)
    return jnp.pad(wg, ((0, 0), (0, 0), (0, LANES - 3 * G_NSA))).reshape(D, H_KV * LANES)


def _regroup_gate_bias(b):
    bg = b.reshape(3, H_KV, G_NSA).transpose(1, 0, 2).reshape(H_KV, 3 * G_NSA)
    return jnp.pad(bg, ((0, 0), (0, LANES - 3 * G_NSA))).reshape(1, H_KV * LANES)


def _layer(l, x, mod, past, W, alpha, t_real):
    B, T, D = x.shape
    N = B * T
    n_main = 10240
    sh1, sc1, g1, sh2, sc2, g2 = [m[:, None, :] for m in jnp.split(mod, 6, axis=-1)]
    u = modulate(x, sc1, sh1, BF16).reshape(N, D)
    tm = 1024
    proj = matmul([u], W['w_in'], l, [0], n_main, tm=tm, tn=512, name="in_proj")
    gate_logits = matmul([u], _gate_weight(W['w_in'], l, n_main), None, [0], H_KV * LANES, tm=tm, tn=512,
                         name="gate_proj")
    bg = _regroup_gate_bias(W['b_gate'][l])
    proj3 = proj.reshape(B, T, n_main)[:, :t_real]
    if past is None:
        sb_rows = state_rows(proj, 1024, 2048).reshape(B, T, 2, H_SB, HEAD_DIM)
        nsa_rows = state_rows(proj, 7168, 2048).reshape(B, T, 4, H_KV, HEAD_DIM)
    else:
        sb_rows = proj3[:, :, 1024:3072].reshape(B, t_real, 2, H_SB, HEAD_DIM)
        nsa_rows = proj3[:, :, 7168:9216].reshape(B, t_real, 4, H_KV, HEAD_DIM)
    win_rows = proj3[:, :, 9216:10240].reshape(B, t_real, 2, H_KV, HEAD_DIM)
    sb_g = W['sb_norm_g'][l][None, :]
    nsa_g = W['nsa_norm_g'][l][None, :]
    conv_args = (W['conv_w'][l], W['conv_b'][l][None, :], W['conv_ln_g'][l][None, :], W['conv_ln_b'][l][None, :])
    if past is None:
        x_sb = sb_prompt(proj, sb_g, B, T)
        hist = jnp.zeros((B, HIST, D // 4), F32)
        x_conv, conv_state = conformer_conv(proj, hist, *conv_args, B, T, min(T, 256))
        kvc = compress_kv(proj, 56, W['cmp_w'], W['cmp_pe'], l, B, T)
        x_nsa = nsa_prompt(proj, kvc, gate_logits, bg, nsa_g, B, T)
        win_state = win_rows[:, -min(WINDOW, T):]
    else:
        cache_sb, nsa_pages, page_table, win_buf, conv_buf, past_len = past
        hist = jnp.pad(conv_buf, ((0, 0), (HIST - (CONV_W - 1), 0), (0, 0)))
        x_conv, conv_state = conformer_conv(proj, hist, *conv_args, B, T, t_real)
        assert past_len % CMP_STRIDE == 0 and t_real < CMP_STRIDE
        glog = gate_logits.reshape(B, T, H_KV, LANES)[:, :t_real, :, :3 * G_NSA].reshape(B, t_real, H_KV, 3, G_NSA)
        glog = glog.transpose(0, 1, 3, 2, 4).reshape(B, t_real, 3 * H_KV * G_NSA)
        win_all = jnp.concatenate([win_buf, win_rows], axis=1)
        win_state = win_all[:, -min(WINDOW, win_all.shape[1]):]
        q_n = proj3[:, :, 5120:7168].reshape(B, t_real, H_KV, G_NSA, HEAD_DIM)
        o_nsa = nsa_decode(q_n, nsa_rows, win_all, glog, W['b_gate'][l], W['nsa_norm_g'][l], nsa_pages, page_table,
                           W['cmp_w'], W['cmp_pe'], l, past_len)
        heads = lambda c0: proj3[:, :, c0:c0 + H_SB * HEAD_DIM].reshape(B, t_real, H_SB, HEAD_DIM)
        o_sb = sb_decode(heads(0), heads(1024), heads(2048), cache_sb, page_table, l, W['sb_norm_g'][l])
        pad_rows = lambda o: jnp.pad(o, ((0, 0), (0, T - t_real), (0, 0))).reshape(N, -1).astype(BF16)
        x_sb, x_nsa = pad_rows(o_sb), pad_rows(o_nsa)
    h = matmul([x_sb, x_conv, x_nsa], W['w_o'], l, [0, 1024, 2048], D, tm=tm, tn=512, name="out_proj")
    w_router = jnp.pad(jnp.concatenate([W['w_rg'][l], W['w_re'][l]], axis=1), ((0, 0), (0, LANES - 20)))
    b_router = jnp.pad(jnp.concatenate([W['b_rg'][l], W['b_re'][l]]), (0, LANES - 20))[None, :]
    x1, x2, rl = post_mixer(x, h.reshape(B, T, D), g1, W['ln1_g'][l][None, :], W['ln1_b'][l][None, :],
                            sc2, sh2, w_router, b_router, alpha)
    f = hier_moe(x2.reshape(N, D), rl.reshape(N, LANES), W['w_gate'], W['w_up'], W['w_down'], l, tm=tm)
    x_out = post_ffn(x1, f.reshape(B, T, D), g2, W['ln2_g'][l][None, :], W['ln2_b'][l][None, :], alpha)
    return x_out, (sb_rows, nsa_rows, win_state, conv_state)


def kernel(x_prompt, x_sample, cache_sb, cache_nsa, state_win, state_conv, page_table, c_prompt, c_sample, w_mod, b_mod, w_in, b_gate, conv_w, conv_b, conv_ln_g, conv_ln_b, cmp_w, cmp_pe, sb_norm_g, nsa_norm_g, w_o, ln1_g, ln1_b, w_rg, b_rg, w_re, b_re, w_gate, w_up, w_down, ln2_g, ln2_b):
    depth = w_mod.shape[0]
    alpha = (2.0 * depth) ** 0.25
    W = dict(w_in=w_in, b_gate=b_gate, conv_w=conv_w, conv_b=conv_b, conv_ln_g=conv_ln_g, conv_ln_b=conv_ln_b,
             cmp_w=cmp_w, cmp_pe=cmp_pe, sb_norm_g=sb_norm_g, nsa_norm_g=nsa_norm_g, w_o=w_o, ln1_g=ln1_g,
             ln1_b=ln1_b, w_rg=w_rg, b_rg=b_rg, w_re=w_re, b_re=b_re, w_gate=w_gate, w_up=w_up, w_down=w_down,
             ln2_g=ln2_g, ln2_b=ln2_b)
    past_len = page_table.shape[1] * cache_sb.shape[2]
    Bp, Bs = c_prompt.shape[0], c_sample.shape[0]
    c_all = jnp.concatenate([c_prompt, c_sample], axis=0)
    c_rows = -(-(Bp + Bs) // 16) * 16
    c_act = jnp.pad(jax.nn.silu(c_all), ((0, c_rows - Bp - Bs), (0, 0)))
    nsa_pages = cache_nsa.reshape(cache_nsa.shape[:2] + (-1, HEAD_DIM))
    t_s = x_sample.shape[1]
    t_pad =-(-t_s // SAMPLE_ROW_TILE) * SAMPLE_ROW_TILE
    xp = x_prompt
    xs = jnp.pad(x_sample, ((0, 0), (0, t_pad - t_s), (0, 0)))
    st_p, st_s = [], []
    for l in range(depth):
        mod = matmul([c_act], w_mod, l, [0], w_mod.shape[2], tm=c_rows, tn=512, bias=b_mod[l][None, :],
                     name="adaln_mod")
        xp, sp = _layer(l, xp, mod[:Bp], None, W, alpha, xp.shape[1])
        past = (cache_sb, nsa_pages, page_table, state_win[l], state_conv[l], past_len)
        xs, ss = _layer(l, xs, mod[Bp:Bp + Bs], past, W, alpha, t_s)
        st_p.append(sp)
        st_s.append(ss)
    sb_p, nsa_p, win_p, conv_p = [jnp.stack([s[i] for s in st_p]) for i in range(4)]
    sb_s, nsa_s, win_s, conv_s = [jnp.stack([s[i] for s in st_s]) for i in range(4)]
    return (xp, xs[:, :t_s], sb_p, nsa_p, win_p, conv_p, sb_s, nsa_s, win_s, conv_s)
```

```python
import functools

import jax
import jax.numpy as jnp
import numpy as np
from jax import lax
from jax.experimental import pallas as pl
from jax.experimental.pallas import tpu as pltpu

F32 = jnp.float32
BF16 = jnp.bfloat16

HEAD_DIM = 128
H_SB = 8
H_KV = 4
G_NSA = 4
CONV_W = 31
L_CMP = 32
CMP_STRIDE = 16
L_SEL = 64
SEL_SHIFT = 6
N_SEL = 16
WINDOW = 512
N_GROUPS = 4
E_PER_GROUP = 4
D_EXPERT = 512
EPS = 1e-5
NEG = -1e30
BIG = 1e9

LANES = 128
SAMPLE_ROW_TILE = 16
VMEM_LIMIT = 56 * 1024 * 1024


def _cparams(sem):
    return pltpu.CompilerParams(dimension_semantics=sem, vmem_limit_bytes=VMEM_LIMIT)


def _modulate_kernel(x_ref, sc_ref, sh_ref, o_ref):
    o_ref[0] = (x_ref[0] * (1.0 + sc_ref[0]) + sh_ref[0]).astype(o_ref.dtype)


def modulate(x, sc, sh, out_dtype):
    B, T, D = x.shape
    tt = min(T, 512)
    return pl.pallas_call(
        _modulate_kernel,
        grid=(B, T // tt),
        in_specs=[pl.BlockSpec((1, tt, D), lambda b, i: (b, i, 0)),
                  pl.BlockSpec((1, 1, D), lambda b, i: (b, 0, 0)),
                  pl.BlockSpec((1, 1, D), lambda b, i: (b, 0, 0))],
        out_specs=pl.BlockSpec((1, tt, D), lambda b, i: (b, i, 0)),
        out_shape=jax.ShapeDtypeStruct((B, T, D), out_dtype),
        compiler_params=_cparams(("parallel", "parallel")),
        name="modulate",
    )(x, sc, sh)


def _mm_kernel(*refs, n_x, has_bias):
    xs = refs[:n_x]
    ws = refs[n_x:2 * n_x]
    o_ref = refs[-1]
    acc = None
    for x_ref, w_ref in zip(xs, ws):
        part = jnp.dot(x_ref[...].astype(BF16), w_ref[...].astype(BF16), preferred_element_type=F32)
        acc = part if acc is None else acc + part
    if has_bias:
        acc = acc + refs[2 * n_x][...]
    o_ref[...] = acc.astype(o_ref.dtype)


def matmul(xs, w, layer, row_offsets, n_cols, *, tm, tn, bias=None, out_dtype=F32, name="matmul"):
    M = xs[0].shape[0]
    tm = min(tm, M)
    assert M % tm == 0 and n_cols % tn == 0
    in_specs = [pl.BlockSpec((tm, x.shape[1]), lambda i, j: (i, 0)) for x in xs]
    for x, off in zip(xs, row_offsets):
        k = x.shape[1]
        assert off % k == 0
        if layer is None:
            in_specs.append(pl.BlockSpec((k, tn), lambda i, j, o=off // k: (o, j)))
        else:
            in_specs.append(pl.BlockSpec((None, k, tn), lambda i, j, o=off // k: (layer, o, j)))
    args = list(xs) + [w] * len(xs)
    if bias is not None:
        in_specs.append(pl.BlockSpec((1, tn), lambda i, j: (0, j)))
        args.append(bias)
    return pl.pallas_call(
        functools.partial(_mm_kernel, n_x=len(xs), has_bias=bias is not None),
        grid=(M // tm, n_cols // tn),
        in_specs=in_specs,
        out_specs=pl.BlockSpec((tm, tn), lambda i, j: (i, j)),
        out_shape=jax.ShapeDtypeStruct((M, n_cols), out_dtype),
        compiler_params=_cparams(("parallel", "arbitrary")),
        name=name,
    )(*args)


def _state_rows_kernel(a_ref, b_ref, o_ref, *, tm, half):
    for j in range(half):
        o_ref[pl.ds(j, tm, stride=2 * half), :] = a_ref[:, j * HEAD_DIM:(j + 1) * HEAD_DIM]
        o_ref[pl.ds(half + j, tm, stride=2 * half), :] = b_ref[:, j * HEAD_DIM:(j + 1) * HEAD_DIM]


def state_rows(proj, col0, width):
    N = proj.shape[0]
    wb = width // 2
    half = wb // HEAD_DIM
    assert col0 % wb == 0
    tm = min(256, N)
    return pl.pallas_call(
        functools.partial(_state_rows_kernel, tm=tm, half=half),
        grid=(N // tm,),
        in_specs=[pl.BlockSpec((tm, wb), lambda i: (i, col0 // wb)),
                  pl.BlockSpec((tm, wb), lambda i: (i, col0 // wb + 1))],
        out_specs=pl.BlockSpec((tm * 2 * half, HEAD_DIM), lambda i: (i, 0)),
        out_shape=jax.ShapeDtypeStruct((N * 2 * half, HEAD_DIM), F32),
        compiler_params=_cparams(("parallel",)),
        name="state_rows",
    )(proj, proj)


def _log_sigmoid(z):
    return jnp.minimum(z, 0.0) - jnp.log1p(jnp.exp(-jnp.abs(z)))


def _sb_prompt_kernel(q_ref, k_ref, v_ref, g_ref, o_ref, *, blk):
    qi = pl.program_id(2)
    scale = HEAD_DIM ** -0.5
    q = q_ref[...].astype(BF16)
    jj = lax.broadcasted_iota(jnp.int32, (blk, blk), 0)
    ss = lax.broadcasted_iota(jnp.int32, (blk, blk), 1)
    later = jnp.where(jj > ss, 1.0, 0.0).astype(BF16)

    def body(i, carry):
        acc, run = carry
        kb = qi - i
        k0 = pl.multiple_of(kb * blk, blk)
        k = k_ref[pl.ds(k0, blk), :].astype(BF16)
        v = v_ref[pl.ds(k0, blk), :].astype(BF16)
        z = lax.dot_general(q, k, (((1,), (1,)), ((), ())), preferred_element_type=F32) * scale
        mask = (kb * blk + ss) < (qi * blk + jj)
        lb = _log_sigmoid(z)
        l1m = jnp.where(mask, lb - z, 0.0)
        hi = l1m.astype(BF16)
        lo = (l1m - hi.astype(F32)).astype(BF16)
        inblk = (jnp.dot(hi, later, preferred_element_type=F32)
                 + jnp.dot(lo, later, preferred_element_type=F32))
        a = jnp.where(mask, jnp.exp(lb + inblk + run), 0.0)
        acc = acc + jnp.dot(a.astype(BF16), v, preferred_element_type=F32)
        run = run + inblk[:, 0:1] + l1m[:, 0:1]
        return acc, run

    acc, _ = lax.fori_loop(0, qi + 1, body,
                           (jnp.zeros((blk, HEAD_DIM), F32), jnp.zeros((blk, 1), F32)))
    ms = jnp.mean(acc * acc, axis=-1, keepdims=True)
    o_ref[...] = (acc * lax.rsqrt(ms + EPS) * g_ref[...]).astype(o_ref.dtype)


def sb_prompt(proj, gain, B, T, *, blk=256):
    blk = min(blk, T)
    nq = T // blk
    return pl.pallas_call(
        functools.partial(_sb_prompt_kernel, blk=blk),
        grid=(B, H_SB, nq),
        in_specs=[pl.BlockSpec((blk, HEAD_DIM), lambda b, h, i: (b * nq + i, h)),
                  pl.BlockSpec((T, HEAD_DIM), lambda b, h, i: (b, H_SB + h)),
                  pl.BlockSpec((T, HEAD_DIM), lambda b, h, i: (b, 2 * H_SB + h)),
                  pl.BlockSpec((1, HEAD_DIM), lambda b, h, i: (0, h))],
        out_specs=pl.BlockSpec((blk, HEAD_DIM), lambda b, h, i: (b * nq + i, h)),
        out_shape=jax.ShapeDtypeStruct((B * T, H_SB * HEAD_DIM), BF16),
        compiler_params=_cparams(("parallel", "parallel", "arbitrary")),
        name="sb_prompt",
    )(proj, proj, proj, gain)


SB_PAGES_PER_STEP = 4
ROWS_PER_TOKEN_SB = 2 * H_SB


def _sb_decode_kernel(pt_ref, qbd_ref, new_ref, tcol_ref, gain_ref, *rest, n_pages_step, page, q_shift):
    page_refs = rest[:n_pages_step]
    o_ref, acc_ref, run_ref = rest[n_pages_step:]
    j = pl.program_id(1)
    scale = HEAD_DIM ** -0.5

    def fold(rows_of, n_keys, masked):
        z = None
        for h in range(H_SB):
            part = jnp.dot(rows_of(0, h).astype(BF16), qbd_ref[0, h * HEAD_DIM:(h + 1) * HEAD_DIM, :].astype(BF16),
                           preferred_element_type=F32)
            z = part if z is None else z + part
        z = z * scale
        lb = _log_sigmoid(z)
        l1m = lb - z
        if masked:
            valid = lax.broadcasted_iota(jnp.int32, (n_keys, LANES), 0) < tcol_ref[...]
            l1m = jnp.where(valid, l1m, 0.0)
        hi = l1m.astype(BF16)
        lo = (l1m - hi.astype(F32)).astype(BF16)
        rs = lax.broadcasted_iota(jnp.int32, (n_keys, n_keys), 0)
        cj = lax.broadcasted_iota(jnp.int32, (n_keys, n_keys), 1)
        later = jnp.where(cj > rs, 1.0, 0.0).astype(BF16)
        inblk = (jnp.dot(later, hi, preferred_element_type=F32) + jnp.dot(later, lo, preferred_element_type=F32))
        a = jnp.exp(lb + inblk + run_ref[...])
        if masked:
            a = jnp.where(valid, a, 0.0)
        run_ref[...] = run_ref[...] + inblk[0:1, :] + l1m[0:1, :]
        at = a.T.astype(BF16)
        for h in range(H_SB):
            acc_ref[h] = acc_ref[h] + jnp.dot(at, rows_of(1, h).astype(BF16), preferred_element_type=F32)

    @pl.when(j == 0)
    def _():
        acc_ref[...] = jnp.zeros(acc_ref.shape, F32)
        run_ref[...] = jnp.zeros(run_ref.shape, F32)
        fold(lambda kv, h: new_ref[0, pl.ds(kv * H_SB + h, LANES, stride=ROWS_PER_TOKEN_SB), :], LANES, True)

    def page_rows(kv, h):
        return jnp.concatenate([r[pl.ds(kv * H_SB + h, page, stride=ROWS_PER_TOKEN_SB), :] for r in page_refs], axis=0)

    fold(page_rows, n_pages_step * page, False)

    @pl.when(j == pl.num_programs(1) - 1)
    def _():
        head_of_row = lax.shift_right_logical(lax.broadcasted_iota(jnp.int32, (LANES, HEAD_DIM), 0), q_shift)
        out = jnp.zeros((LANES, HEAD_DIM), F32)
        for h in range(H_SB):
            out = out + jnp.where(head_of_row == h, acc_ref[h], 0.0)
        ms = jnp.mean(out * out, axis=-1, keepdims=True)
        o_ref[0] = out * lax.rsqrt(ms + EPS) * gain_ref[...]


def sb_decode(q, k_new, v_new, cache, page_table, layer, gain):
    B, Tq, H, hd = q.shape
    depth, n_pool, page = cache.shape[:3]
    n_pages = page_table.shape[1]
    P = SB_PAGES_PER_STEP
    q_shift = Tq.bit_length() - 1
    assert Tq == 1 << q_shift and H * Tq <= LANES and Tq <= LANES and n_pages % P == 0
    eye = jnp.eye(H, dtype=F32)
    qbd = (q.transpose(0, 2, 3, 1)[:, :, :, None, :] * eye[None, :, None, :, None]).reshape(B, H * hd, H * Tq)
    qbd = jnp.pad(qbd, ((0, 0), (0, 0), (0, LANES - H * Tq)))
    new = jnp.stack([k_new, v_new], axis=2).reshape(B, Tq * ROWS_PER_TOKEN_SB, hd)
    new = jnp.pad(new, ((0, 0), (0, (LANES - Tq) * ROWS_PER_TOKEN_SB), (0, 0)))
    lane = jnp.arange(LANES, dtype=jnp.int32)
    tcol = jnp.where(lane < H * Tq, lane % Tq, -1)[None, :]
    gain_rows = jnp.pad(jnp.repeat(gain.reshape(H, hd), Tq, axis=0), ((0, LANES - H * Tq), (0, 0)),
                        constant_values=1.0)
    pages = cache.reshape(depth, n_pool, page * ROWS_PER_TOKEN_SB, hd)
    page_spec = lambda r: pl.BlockSpec(
        (None, None, page * ROWS_PER_TOKEN_SB, hd),
        lambda b, j, pt: (layer, pt[b, n_pages - (j + 1) * P + r], 0, 0))
    out = pl.pallas_call(
        functools.partial(_sb_decode_kernel, n_pages_step=P, page=page, q_shift=q_shift),
        grid_spec=pltpu.PrefetchScalarGridSpec(
            num_scalar_prefetch=1, grid=(B, n_pages // P),
            in_specs=[pl.BlockSpec((1, H * hd, LANES), lambda b, j, pt: (b, 0, 0)),
                      pl.BlockSpec((1, LANES * ROWS_PER_TOKEN_SB, hd), lambda b, j, pt: (b, 0, 0)),
                      pl.BlockSpec((1, LANES), lambda b, j, pt: (0, 0)),
                      pl.BlockSpec((LANES, hd), lambda b, j, pt: (0, 0))] + [page_spec(r) for r in range(P)],
            out_specs=pl.BlockSpec((1, LANES, hd), lambda b, j, pt: (b, 0, 0)),
            scratch_shapes=[pltpu.VMEM((H, LANES, hd), F32), pltpu.VMEM((1, LANES), F32)]),
        out_shape=jax.ShapeDtypeStruct((B, LANES, hd), F32),
        compiler_params=_cparams(("parallel", "arbitrary")),
        name="sb_decode",
    )(page_table, qbd, new, tcol, gain_rows, *([pages] * P))
    return out[:, :H * Tq].reshape(B, H, Tq, hd).transpose(0, 2, 1, 3).reshape(B, Tq, H * hd)


HIST = 32


def _conv_kernel(a_ref, b_ref, hist_ref, w_ref, cb_ref, g_ref, beta_ref, y_ref, st_ref, ext_ref, *, tb, t_valid):
    i = pl.program_id(1)

    @pl.when(i == 0)
    def _():
        ext_ref[0:HIST, :] = hist_ref[0]

    u = a_ref[...] * jax.nn.sigmoid(b_ref[...])
    ext_ref[HIST:HIST + tb, :] = u
    acc = jnp.zeros(u.shape, F32)
    for k in range(CONV_W):
        acc = acc + ext_ref[pl.ds(HIST - (CONV_W - 1) + k, tb), :] * w_ref[k:k + 1, :]
    y = acc + cb_ref[...]
    mu = jnp.mean(y, axis=-1, keepdims=True)
    var = jnp.mean(jnp.square(y - mu), axis=-1, keepdims=True)
    yn = (y - mu) * lax.rsqrt(var + EPS) * g_ref[...] + beta_ref[...]
    y_ref[...] = (yn * jax.nn.sigmoid(yn)).astype(y_ref.dtype)
    st_ref[0] = ext_ref[pl.ds(HIST + t_valid - (CONV_W - 1), CONV_W - 1), :]
    if tb >= HIST:
        ext_ref[0:HIST, :] = ext_ref[tb:tb + HIST, :]


def conformer_conv(proj, hist, conv_w, conv_b, ln_g, ln_b, B, T, t_valid, *, tb=256):
    C = conv_w.shape[-1]
    tb = min(tb, T)
    nt = T // tb
    assert nt == 1 or (tb >= HIST and t_valid == tb)
    return pl.pallas_call(
        functools.partial(_conv_kernel, tb=tb, t_valid=t_valid),
        grid=(B, nt),
        in_specs=[pl.BlockSpec((tb, C), lambda b, i: (b * nt + i, 3)),
                  pl.BlockSpec((tb, C), lambda b, i: (b * nt + i, 4)),
                  pl.BlockSpec((1, HIST, C), lambda b, i: (b, 0, 0)),
                  pl.BlockSpec((CONV_W, C), lambda b, i: (0, 0)),
                  pl.BlockSpec((1, C), lambda b, i: (0, 0)),
                  pl.BlockSpec((1, C), lambda b, i: (0, 0)),
                  pl.BlockSpec((1, C), lambda b, i: (0, 0))],
        out_specs=[pl.BlockSpec((tb, C), lambda b, i: (b * nt + i, 0)),
                   pl.BlockSpec((1, CONV_W - 1, C), lambda b, i: (b, 0, 0))],
        out_shape=[jax.ShapeDtypeStruct((B * T, C), BF16),
                   jax.ShapeDtypeStruct((B, CONV_W - 1, C), F32)],
        scratch_shapes=[pltpu.VMEM((HIST + tb, C), F32)],
        compiler_params=_cparams(("parallel", "arbitrary")),
        name="conformer_conv",
    )(proj, proj, hist, conv_w, conv_b, ln_g, ln_b)


def _compress_kernel(x_ref, w_ref, pe_ref, o_ref, *, n_chunks):
    parts = []
    for r in range(L_CMP // CMP_STRIDE):
        p = jnp.zeros((n_chunks, HEAD_DIM), F32)
        for l in range(CMP_STRIDE):
            j = r * CMP_STRIDE + l
            rows = x_ref[pl.ds(l, n_chunks, stride=CMP_STRIDE), :] + pe_ref[j:j + 1, :]
            p = p + jnp.dot(rows.astype(BF16), w_ref[j].astype(BF16), preferred_element_type=F32)
        parts.append(p)
    o_ref[...] = parts[0] + pltpu.roll(parts[1], n_chunks - 1, 0)


def compress_kv(rows, col_block0, cmp_w, cmp_pe, layer, B, T):
    n_chunks = T // CMP_STRIDE
    return pl.pallas_call(
        functools.partial(_compress_kernel, n_chunks=n_chunks),
        grid=(B, 2, H_KV),
        in_specs=[pl.BlockSpec((T, HEAD_DIM), lambda b, ty, h: (b, col_block0 + ty * H_KV + h)),
                  pl.BlockSpec((None, None, L_CMP, HEAD_DIM, HEAD_DIM), lambda b, ty, h: (layer, ty, 0, 0, 0)),
                  pl.BlockSpec((None, None, L_CMP, HEAD_DIM), lambda b, ty, h: (layer, ty, 0, 0))],
        out_specs=pl.BlockSpec((None, None, None, n_chunks, HEAD_DIM), lambda b, ty, h: (b, ty, h, 0, 0)),
        out_shape=jax.ShapeDtypeStruct((B, 2, H_KV, n_chunks, HEAD_DIM), F32),
        compiler_params=_cparams(("parallel", "parallel", "parallel")),
        name="nsa_compress",
    )(rows, cmp_w, cmp_pe)


def _nsa_prompt_kernel(q_ref, kc_ref, vc_ref, sk_ref, sv_ref, wk_ref, wv_ref, gl_ref, bg_ref, gain_ref,
                       o_ref, *, tq, bk, n_sel):
    qi = pl.program_id(2)
    scale = HEAD_DIM ** -0.5
    rows = G_NSA * tq
    q0 = qi * tq
    q = jnp.concatenate([q_ref[:, g * HEAD_DIM:(g + 1) * HEAD_DIM] for g in range(G_NSA)], axis=0).astype(BF16)
    nt = (((1,), (1,)), ((), ()))

    n_cmp_pad = kc_ref.shape[0]
    s = lax.dot_general(q, kc_ref[...].astype(BF16), nt, preferred_element_type=F32) * scale
    cl = lax.broadcasted_iota(jnp.int32, (rows, n_cmp_pad), 1)
    qp = q0 + (lax.broadcasted_iota(jnp.int32, (rows, n_cmp_pad), 0) & (tq - 1))
    cmask = (cl * CMP_STRIDE + (L_CMP - 1)) <= qp
    sm = jnp.where(cmask, s, NEG)
    e = jnp.where(cmask, jnp.exp(sm - jnp.max(sm, axis=-1, keepdims=True)), 0.0)
    den = jnp.sum(e, axis=-1, keepdims=True)
    p = e / jnp.where(den > 0.0, den, 1.0)
    pb = p.astype(BF16)
    o_cmp = jnp.dot(pb, vc_ref[...].astype(BF16), preferred_element_type=F32)

    cn = lax.broadcasted_iota(jnp.int32, (n_cmp_pad, LANES), 0) * CMP_STRIDE
    sj = lax.broadcasted_iota(jnp.int32, (n_cmp_pad, LANES), 1) * L_SEL
    ov = jnp.clip(jnp.minimum(cn + L_CMP, sj + L_SEL) - jnp.maximum(cn, sj), 0, None).astype(F32) / L_CMP
    ov = ov.astype(BF16)
    imp = None
    for g in range(G_NSA):
        part = jnp.dot(pb[g * tq:(g + 1) * tq], ov, preferred_element_type=F32)
        imp = part if imp is None else imp + part
    blk = lax.broadcasted_iota(jnp.int32, (tq, LANES), 1)
    qpt = q0 + lax.broadcasted_iota(jnp.int32, (tq, LANES), 0)
    cur = jnp.right_shift(qpt, SEL_SHIFT)
    forced = (blk == 0) | (blk == cur) | (blk == cur - 1)
    valid = blk * L_SEL <= qpt
    score = jnp.where(forced, BIG, jnp.where(valid, imp, -BIG))
    score = jnp.where(blk < n_sel, score, -3e38)
    rank = jnp.zeros((tq, LANES), jnp.int32)
    for i in range(n_sel):
        ci = score[:, i:i + 1]
        ahead = (ci > score) | ((ci == score) & (blk > i))
        rank = rank + ahead.astype(jnp.int32)
    sel = jnp.where((rank < min(N_SEL, n_sel)) & (blk < n_sel), 1.0, 0.0).astype(BF16)

    kj = lax.broadcasted_iota(jnp.int32, (LANES, bk), 0)
    ks = lax.broadcasted_iota(jnp.int32, (LANES, bk), 1)
    klane = lax.broadcasted_iota(jnp.int32, (rows, bk), 1)
    qpos = q0 + (lax.broadcasted_iota(jnp.int32, (rows, bk), 0) & (tq - 1))

    def flash_step(k_ref, v_ref, kb, allowed_fn, carry):
        m, l, acc = carry
        k0 = pl.multiple_of(kb * bk, bk)
        k = k_ref[pl.ds(k0, bk), :].astype(BF16)
        v = v_ref[pl.ds(k0, bk), :].astype(BF16)
        sc = lax.dot_general(q, k, nt, preferred_element_type=F32) * scale
        allowed = allowed_fn(kb * bk + klane)
        scm = jnp.where(allowed, sc, NEG)
        m_new = jnp.maximum(m, jnp.max(scm, axis=-1, keepdims=True))
        alpha = jnp.exp(m - m_new)
        pe = jnp.where(allowed, jnp.exp(scm - m_new), 0.0)
        l = alpha * l + jnp.sum(pe, axis=-1, keepdims=True)
        acc = alpha * acc + jnp.dot(pe.astype(BF16), v, preferred_element_type=F32)
        return m_new, l, acc

    init = (jnp.full((rows, 1), NEG, F32), jnp.zeros((rows, 1), F32), jnp.zeros((rows, HEAD_DIM), F32))

    def slc_body(kb, carry):
        expand = jnp.where(jnp.right_shift(kb * bk + ks, SEL_SHIFT) == kj, 1.0, 0.0).astype(BF16)
        selk = jnp.dot(sel, expand, preferred_element_type=F32)
        selk = jnp.concatenate([selk] * G_NSA, axis=0)
        return flash_step(sk_ref, sv_ref, kb, lambda kpos: (selk > 0.5) & (kpos <= qpos), carry)

    last_kb = (q0 + tq - 1) // bk
    _, l_s, acc_s = lax.fori_loop(0, last_kb + 1, slc_body, init)
    o_slc = acc_s / l_s

    def win_body(kb, carry):
        return flash_step(wk_ref, wv_ref, kb,
                          lambda kpos: (qpos - kpos >= 0) & (qpos - kpos < WINDOW), carry)

    first_kb = jnp.maximum(q0 - (WINDOW - 1), 0) // bk
    _, l_w, acc_w = lax.fori_loop(first_kb, last_kb + 1, win_body, init)
    o_win = acc_w / l_w

    gates = jax.nn.sigmoid(gl_ref[...] + bg_ref[...])
    for g in range(G_NSA):
        r = slice(g * tq, (g + 1) * tq)
        o = (gates[:, g:g + 1] * o_cmp[r] + gates[:, G_NSA + g:G_NSA + g + 1] * o_slc[r]
             + gates[:, 2 * G_NSA + g:2 * G_NSA + g + 1] * o_win[r])
        ms = jnp.mean(o * o, axis=-1, keepdims=True)
        o = o * lax.rsqrt(ms + EPS) * gain_ref[:, g * HEAD_DIM:(g + 1) * HEAD_DIM]
        o_ref[:, g * HEAD_DIM:(g + 1) * HEAD_DIM] = o.astype(o_ref.dtype)


def nsa_prompt(proj, kvc, gate_logits, b_gate, gain, B, T):
    tq = LANES
    bk = min(2 * LANES, T)
    assert T % tq == 0 and T % bk == 0 and bk % L_SEL == 0
    nq = T // tq
    n_sel = -(-T // L_SEL)
    assert n_sel <= LANES
    n_cmp_pad = kvc.shape[3]
    qw = G_NSA * HEAD_DIM
    seq = lambda c0: pl.BlockSpec((T, HEAD_DIM), lambda b, h, i: (b, c0 + h))
    return pl.pallas_call(
        functools.partial(_nsa_prompt_kernel, tq=tq, bk=bk, n_sel=n_sel),
        grid=(B, H_KV, nq),
        in_specs=[pl.BlockSpec((tq, qw), lambda b, h, i: (b * nq + i, 5120 // qw + h)),
                  pl.BlockSpec((None, None, None, n_cmp_pad, HEAD_DIM), lambda b, h, i: (b, 0, h, 0, 0)),
                  pl.BlockSpec((None, None, None, n_cmp_pad, HEAD_DIM), lambda b, h, i: (b, 1, h, 0, 0)),
                  seq(64), seq(68), seq(72), seq(76),
                  pl.BlockSpec((tq, LANES), lambda b, h, i: (b * nq + i, h)),
                  pl.BlockSpec((1, LANES), lambda b, h, i: (0, h)),
                  pl.BlockSpec((1, qw), lambda b, h, i: (0, h))],
        out_specs=pl.BlockSpec((tq, qw), lambda b, h, i: (b * nq + i, h)),
        out_shape=jax.ShapeDtypeStruct((B * T, H_KV * qw), BF16),
        compiler_params=_cparams(("parallel", "parallel", "arbitrary")),
        name="nsa_prompt",
    )(proj, kvc, kvc, proj, proj, proj, proj, gate_logits, b_gate, gain)


def _layer_norm(y, g, b):
    mu = jnp.mean(y, axis=-1, keepdims=True)
    var = jnp.mean(jnp.square(y - mu), axis=-1, keepdims=True)
    return (y - mu) * lax.rsqrt(var + EPS) * g + b


def _ln1_kernel(x_ref, h_ref, g1_ref, lg_ref, lb_ref, sc_ref, sh_ref, wr_ref, br_ref, x1_ref, x2_ref, rl_ref, *, alpha):
    x1 = _layer_norm(alpha * x_ref[0] + g1_ref[0] * h_ref[0], lg_ref[...], lb_ref[...])
    x1_ref[0] = x1
    x2 = x1 * (1.0 + sc_ref[0]) + sh_ref[0]
    x2_ref[0] = x2.astype(x2_ref.dtype)
    rl_ref[0] = jnp.dot(x2.astype(BF16), wr_ref[...].astype(BF16), preferred_element_type=F32) + br_ref[...]


def post_mixer(x, h, g1, ln_g, ln_b, sc2, sh2, w_router, b_router, alpha):
    B, T, D = x.shape
    tt = min(T, 256)
    row = pl.BlockSpec((1, tt, D), lambda b, i: (b, i, 0))
    per_b = pl.BlockSpec((1, 1, D), lambda b, i: (b, 0, 0))
    vec = pl.BlockSpec((1, D), lambda b, i: (0, 0))
    return pl.pallas_call(
        functools.partial(_ln1_kernel, alpha=alpha),
        grid=(B, T // tt),
        in_specs=[row, row, per_b, vec, vec, per_b, per_b,
                  pl.BlockSpec((D, LANES), lambda b, i: (0, 0)),
                  pl.BlockSpec((1, LANES), lambda b, i: (0, 0))],
        out_specs=[row, row, pl.BlockSpec((1, tt, LANES), lambda b, i: (b, i, 0))],
        out_shape=[jax.ShapeDtypeStruct((B, T, D), F32), jax.ShapeDtypeStruct((B, T, D), BF16),
                   jax.ShapeDtypeStruct((B, T, LANES), F32)],
        compiler_params=_cparams(("parallel", "parallel")),
        name="post_mixer_ln",
    )(x, h, g1, ln_g, ln_b, sc2, sh2, w_router, b_router)


def _ln2_kernel(x_ref, f_ref, g2_ref, lg_ref, lb_ref, o_ref, *, alpha):
    o_ref[0] = _layer_norm(alpha * x_ref[0] + g2_ref[0] * f_ref[0], lg_ref[...], lb_ref[...])


def post_ffn(x, f, g2, ln_g, ln_b, alpha):
    B, T, D = x.shape
    tt = min(T, 256)
    row = pl.BlockSpec((1, tt, D), lambda b, i: (b, i, 0))
    return pl.pallas_call(
        functools.partial(_ln2_kernel, alpha=alpha),
        grid=(B, T // tt),
        in_specs=[row, row, pl.BlockSpec((1, 1, D), lambda b, i: (b, 0, 0)),
                  pl.BlockSpec((1, D), lambda b, i: (0, 0)), pl.BlockSpec((1, D), lambda b, i: (0, 0))],
        out_specs=row,
        out_shape=jax.ShapeDtypeStruct((B, T, D), F32),
        compiler_params=_cparams(("parallel", "parallel")),
        name="post_ffn_ln",
    )(x, f, g2, ln_g, ln_b)


def _moe_up_kernel(x_ref, wg_ref, wu_ref, cw_ref, h_ref):
    x = x_ref[...]
    hg = jnp.dot(x, wg_ref[...].astype(BF16), preferred_element_type=F32)
    hu = jnp.dot(x, wu_ref[...].astype(BF16), preferred_element_type=F32)
    h_ref[...] = (hg * jax.nn.sigmoid(hg) * hu * cw_ref[...]).astype(h_ref.dtype)


def _mm_acc_kernel(x_ref, w_ref, o_ref, acc_ref):
    k = pl.program_id(2)

    @pl.when(k == 0)
    def _():
        acc_ref[...] = jnp.zeros(acc_ref.shape, F32)

    acc_ref[...] += jnp.dot(x_ref[...].astype(BF16), w_ref[...].astype(BF16), preferred_element_type=F32)

    @pl.when(k == pl.num_programs(2) - 1)
    def _():
        o_ref[...] = acc_ref[...]


def hier_moe(x2, rl, w_gate, w_up, w_down, layer, *, tm):
    N, D = x2.shape
    E = N_GROUPS * E_PER_GROUP
    F = D_EXPERT
    g_prob = jax.nn.softmax(rl[:, :N_GROUPS], axis=-1)
    g_p, g_idx = lax.top_k(g_prob, 1)
    e_logits = rl[:, N_GROUPS:N_GROUPS + E].reshape(N, N_GROUPS, E_PER_GROUP)
    e_logits = jnp.take_along_axis(e_logits, g_idx[:, :, None], axis=1)[:, 0]
    e_p, e_idx = lax.top_k(jax.nn.softmax(e_logits, axis=-1), 2)
    w = g_p * e_p / jnp.sum(e_p, -1, keepdims=True)
    expert = g_idx * E_PER_GROUP + e_idx
    combine = jnp.sum(jax.nn.one_hot(expert, E, dtype=F32) * w[..., None], axis=1)
    tm = min(tm, N)
    fc = F // 2
    per = F // fc
    w_spec = pl.BlockSpec((None, None, D, fc), lambda i, c: (layer, c // per, 0, c % per))
    h = pl.pallas_call(
        _moe_up_kernel,
        grid=(N // tm, E * per),
        in_specs=[pl.BlockSpec((tm, D), lambda i, c: (i, 0)), w_spec, w_spec,
                  pl.BlockSpec((None, tm, 1), lambda i, c: (c // per, i, 0))],
        out_specs=pl.BlockSpec((tm, fc), lambda i, c: (i, c)),
        out_shape=jax.ShapeDtypeStruct((N, E * F), BF16),
        compiler_params=_cparams(("parallel", "arbitrary")),
        name="moe_gate_up",
    )(x2, w_gate, w_up, combine.T[:, :, None])
    wd = w_down.reshape(w_down.shape[0], E * F, D)
    tn = tk = 1024
    return pl.pallas_call(
        _mm_acc_kernel,
        grid=(N // tm, D // tn, E * F // tk),
        in_specs=[pl.BlockSpec((tm, tk), lambda i, j, k: (i, k)),
                  pl.BlockSpec((None, tk, tn), lambda i, j, k: (layer, k, j))],
        out_specs=pl.BlockSpec((tm, tn), lambda i, j, k: (i, j)),
        out_shape=jax.ShapeDtypeStruct((N, D), F32),
        scratch_shapes=[pltpu.VMEM((tm, tn), F32)],
        compiler_params=_cparams(("parallel", "parallel", "arbitrary")),
        name="moe_down",
    )(h, wd)


ROWS_PER_TOKEN_NSA = 4 * H_KV
CMP_PAGES_PER_STEP = 16
NSA_PAGES_PER_STEP = 4
Q_ROWS = 8


def _compress_paged_kernel(pt_ref, w_ref, pe_ref, *rest, n_pages_step, page):
    page_refs = rest[:n_pages_step]
    o_ref = rest[n_pages_step]
    cpp = page // CMP_STRIDE
    kvh = 2 * H_KV
    m = n_pages_step * cpp * kvh
    is_v = (lax.broadcasted_iota(jnp.int32, (m, HEAD_DIM), 0) & (kvh - 1)) >= H_KV
    for r in range(L_CMP // CMP_STRIDE):
        acc = jnp.zeros((m, HEAD_DIM), F32)
        for l in range(CMP_STRIDE):
            j = r * CMP_STRIDE + l
            rows = jnp.concatenate([p[:, l].reshape(cpp * kvh, HEAD_DIM) for p in page_refs], axis=0)
            rows = (rows + jnp.where(is_v, pe_ref[1, j:j + 1, :], pe_ref[0, j:j + 1, :])).astype(BF16)
            w_kv = jnp.concatenate([w_ref[0, j], w_ref[1, j]], axis=1).astype(BF16)
            both = jnp.dot(rows, w_kv, preferred_element_type=F32)
            acc = acc + jnp.where(is_v, both[:, HEAD_DIM:], both[:, :HEAD_DIM])
        o_ref[r] = acc


def compress_paged(pages, page_table, cmp_w, cmp_pe, layer):
    depth, n_pool, prow, hd = pages.shape
    page = prow // ROWS_PER_TOKEN_NSA
    B, n_pages = page_table.shape
    P = CMP_PAGES_PER_STEP
    assert n_pages % P == 0 and page % CMP_STRIDE == 0
    cpp = page // CMP_STRIDE
    kvh = 2 * H_KV
    chunks = pages.reshape(depth, n_pool, cpp, CMP_STRIDE, ROWS_PER_TOKEN_NSA, hd)
    page_spec = lambda r: pl.BlockSpec((None, None, cpp, CMP_STRIDE, kvh, hd),
                                       lambda b, j, pt: (layer, pt[b, j * P + r], 0, 0, 0, 0))
    return pl.pallas_call(
        functools.partial(_compress_paged_kernel, n_pages_step=P, page=page),
        grid_spec=pltpu.PrefetchScalarGridSpec(
            num_scalar_prefetch=1, grid=(B, n_pages // P),
            in_specs=[pl.BlockSpec((None, 2, L_CMP, hd, hd), lambda b, j, pt: (layer, 0, 0, 0, 0)),
                      pl.BlockSpec((None, 2, L_CMP, hd), lambda b, j, pt: (layer, 0, 0, 0))]
            + [page_spec(r) for r in range(P)],
            out_specs=pl.BlockSpec((None, 2, P * cpp * kvh, hd), lambda b, j, pt: (b, 0, j, 0))),
        out_shape=jax.ShapeDtypeStruct((B, 2, n_pages * cpp * kvh, hd), F32),
        compiler_params=_cparams(("parallel", "arbitrary")),
        name="nsa_compress_paged",
    )(page_table, cmp_w, cmp_pe, *([chunks] * P))


def _nsa_decode_front_kernel(q_ref, pk_ref, pv_ref, wk_ref, wv_ref, ocmp_ref, owin_ref, sel_ref, *,
                             past_len, n_win, n_sel, sel_w):
    scale = HEAD_DIM ** -0.5
    rows = G_NSA * Q_ROWS
    nt = (((1,), (1,)), ((), ()))
    q = q_ref[...].astype(BF16)
    n_cmp = pk_ref.shape[1]
    kc = pk_ref[0] + pltpu.roll(pk_ref[1], n_cmp - 1, 0)
    vc = pv_ref[0] + pltpu.roll(pv_ref[1], n_cmp - 1, 0)
    s = lax.dot_general(q, kc.astype(BF16), nt, preferred_element_type=F32) * scale
    cl = lax.broadcasted_iota(jnp.int32, (rows, n_cmp), 1)
    qp = past_len + (lax.broadcasted_iota(jnp.int32, (rows, n_cmp), 0) & (Q_ROWS - 1))
    cmask = (cl * CMP_STRIDE + (L_CMP - 1)) <= qp
    sm = jnp.where(cmask, s, NEG)
    e = jnp.where(cmask, jnp.exp(sm - jnp.max(sm, axis=-1, keepdims=True)), 0.0)
    den = jnp.sum(e, axis=-1, keepdims=True)
    pb = (e / jnp.where(den > 0.0, den, 1.0)).astype(BF16)
    ocmp_ref[...] = jnp.dot(pb, vc.astype(BF16), preferred_element_type=F32)
    cn = lax.broadcasted_iota(jnp.int32, (n_cmp, sel_w), 0) * CMP_STRIDE
    sj = lax.broadcasted_iota(jnp.int32, (n_cmp, sel_w), 1) * L_SEL
    ov = (jnp.clip(jnp.minimum(cn + L_CMP, sj + L_SEL) - jnp.maximum(cn, sj), 0, None).astype(F32) / L_CMP).astype(BF16)
    imp = None
    for g in range(G_NSA):
        part = jnp.dot(pb[g * Q_ROWS:(g + 1) * Q_ROWS], ov, preferred_element_type=F32)
        imp = part if imp is None else imp + part
    blk = lax.broadcasted_iota(jnp.int32, (Q_ROWS, sel_w), 1)
    qpt = past_len + lax.broadcasted_iota(jnp.int32, (Q_ROWS, sel_w), 0)
    cur = jnp.right_shift(qpt, SEL_SHIFT)
    forced = (blk == 0) | (blk == cur) | (blk == cur - 1)
    valid = blk * L_SEL <= qpt
    score = jnp.where(forced, BIG, jnp.where(valid, imp, -BIG))
    score = jnp.where(blk < n_sel, score, -3e38)
    rank = jnp.zeros((Q_ROWS, sel_w), jnp.int32)
    for i in range(n_sel):
        ci = score[:, i:i + 1]
        ahead = (ci > score) | ((ci == score) & (blk > i))
        rank = rank + ahead.astype(jnp.int32)
    sel_ref[...] = jnp.where((rank < min(N_SEL, n_sel)) & (blk < n_sel), 1.0, 0.0)
    n_wpad = wk_ref.shape[0]
    sw = lax.dot_general(q, wk_ref[...].astype(BF16), nt, preferred_element_type=F32) * scale
    wi = lax.broadcasted_iota(jnp.int32, (rows, n_wpad), 1)
    wq = past_len + (lax.broadcasted_iota(jnp.int32, (rows, n_wpad), 0) & (Q_ROWS - 1))
    wpos = wi + (past_len - n_win)
    d = wq - wpos
    wmask = (d >= 0) & (d < WINDOW) & (wpos >= 0)
    swm = jnp.where(wmask, sw, NEG)
    ew = jnp.where(wmask, jnp.exp(swm - jnp.max(swm, axis=-1, keepdims=True)), 0.0)
    pw = ew / jnp.sum(ew, axis=-1, keepdims=True)
    owin_ref[...] = jnp.dot(pw.astype(BF16), wv_ref[...].astype(BF16), preferred_element_type=F32)


def _nsa_decode_sweep_kernel(pt_ref, q_ref, sel_ref, new_ref, ocmp_ref, owin_ref, gl_ref, gb_ref, gain_ref, *rest,
                             n_pages_step, page, past_len, sel_w):
    page_refs = rest[:n_pages_step]
    o_ref, m_ref, l_ref, acc_ref = rest[n_pages_step:]
    j = pl.program_id(1)
    scale = HEAD_DIM ** -0.5
    rows = G_NSA * Q_ROWS
    nt = (((1,), (1,)), ((), ()))

    def fold(rows_of, n_keys, pos0):
        kpos = pos0 + lax.broadcasted_iota(jnp.int32, (rows, n_keys), 1)
        qpos = past_len + (lax.broadcasted_iota(jnp.int32, (rows, n_keys), 0) & (Q_ROWS - 1))
        eb = lax.broadcasted_iota(jnp.int32, (sel_w, n_keys), 0)
        ek = pos0 + lax.broadcasted_iota(jnp.int32, (sel_w, n_keys), 1)
        expand = jnp.where(jnp.right_shift(ek, SEL_SHIFT) == eb, 1.0, 0.0).astype(BF16)
        for h in range(H_KV):
            r = slice(h * rows, (h + 1) * rows)
            k = rows_of(2 * H_KV + h).astype(BF16)
            v = rows_of(3 * H_KV + h).astype(BF16)
            s = lax.dot_general(q_ref[r, :].astype(BF16), k, nt, preferred_element_type=F32) * scale
            selk = jnp.dot(sel_ref[h].astype(BF16), expand, preferred_element_type=F32)
            selk = jnp.concatenate([selk] * G_NSA, axis=0)
            allowed = (selk > 0.5) & (kpos <= qpos)
            sm = jnp.where(allowed, s, NEG)
            m_old = m_ref[r, :]
            m_new = jnp.maximum(m_old, jnp.max(sm, axis=-1, keepdims=True))
            alpha = jnp.exp(m_old - m_new)
            pe = jnp.where(allowed, jnp.exp(sm - m_new), 0.0)
            l_ref[r, :] = alpha * l_ref[r, :] + jnp.sum(pe, axis=-1, keepdims=True)
            acc_ref[r, :] = alpha * acc_ref[r, :] + jnp.dot(pe.astype(BF16), v, preferred_element_type=F32)
            m_ref[r, :] = m_new

    @pl.when(j == 0)
    def _():
        m_ref[...] = jnp.full(m_ref.shape, NEG, F32)
        l_ref[...] = jnp.zeros(l_ref.shape, F32)
        acc_ref[...] = jnp.zeros(acc_ref.shape, F32)
        fold(lambda c: new_ref[pl.ds(c, LANES, stride=ROWS_PER_TOKEN_NSA), :], LANES, past_len)

    def page_rows(c):
        return jnp.concatenate([p[pl.ds(c, page, stride=ROWS_PER_TOKEN_NSA), :] for p in page_refs], axis=0)

    fold(page_rows, n_pages_step * page, j * (n_pages_step * page))

    @pl.when(j == pl.num_programs(1) - 1)
    def _():
        gates = jax.nn.sigmoid(gl_ref[...] + gb_ref[...])
        o = (gates[:, 0:1] * ocmp_ref[...] + gates[:, 1:2] * (acc_ref[...] / l_ref[...])
             + gates[:, 2:3] * owin_ref[...])
        ms = jnp.mean(o * o, axis=-1, keepdims=True)
        o_ref[...] = o * lax.rsqrt(ms + EPS) * gain_ref[...]


def nsa_decode(q_n, nsa_new, win_all, gate_logits, b_gate, gain, pages, page_table, cmp_w, cmp_pe, layer, past_len):
    B, T, H, G, hd = q_n.shape
    assert T <= Q_ROWS and H == H_KV and G == G_NSA
    rows = G * Q_ROWS
    page = pages.shape[2] // ROWS_PER_TOKEN_NSA
    n_pages = page_table.shape[1]
    n_sel = -(-(past_len + T) // L_SEL)
    sel_w = -(-n_sel // LANES) * LANES
    n_win = win_all.shape[1] - T
    n_wpad = -(-(n_win + Q_ROWS) // LANES) * LANES
    kvc = compress_paged(pages, page_table, cmp_w, cmp_pe, layer)
    n_cmp = kvc.shape[2] // (2 * H)
    kvc = kvc.reshape(B, 2, n_cmp, 2 * H, hd).transpose(0, 1, 3, 2, 4)
    tpad = lambda a, axis: jnp.pad(a, [(0, Q_ROWS - T) if i == axis else (0, 0) for i in range(a.ndim)])
    q_rows = tpad(q_n.transpose(0, 2, 3, 1, 4), 3).reshape(B, H, rows, hd)
    win_t = win_all.transpose(0, 2, 3, 1, 4)
    win_t = jnp.pad(win_t, ((0, 0), (0, 0), (0, 0), (0, n_wpad - n_win - T), (0, 0)))
    per_bh = lambda *shape: pl.BlockSpec((None, None) + shape, lambda b, h: (b, h) + (0,) * len(shape))
    o_cmp, o_win, sel = pl.pallas_call(
        functools.partial(_nsa_decode_front_kernel, past_len=past_len, n_win=n_win, n_sel=n_sel, sel_w=sel_w),
        grid=(B, H),
        in_specs=[per_bh(rows, hd),
                  pl.BlockSpec((None, 2, None, n_cmp, hd), lambda b, h: (b, 0, h, 0, 0)),
                  pl.BlockSpec((None, 2, None, n_cmp, hd), lambda b, h: (b, 0, H + h, 0, 0)),
                  pl.BlockSpec((None, None, None, n_wpad, hd), lambda b, h: (b, 0, h, 0, 0)),
                  pl.BlockSpec((None, None, None, n_wpad, hd), lambda b, h: (b, 1, h, 0, 0))],
        out_specs=[per_bh(rows, hd), per_bh(rows, hd), per_bh(Q_ROWS, sel_w)],
        out_shape=[jax.ShapeDtypeStruct((B, H, rows, hd), F32), jax.ShapeDtypeStruct((B, H, rows, hd), F32),
                   jax.ShapeDtypeStruct((B, H, Q_ROWS, sel_w), F32)],
        compiler_params=_cparams(("parallel", "parallel")),
        name="nsa_decode_front",
    )(q_rows, kvc, kvc, win_t, win_t)
    gl = gate_logits.reshape(B, T, 3, H, G).transpose(0, 3, 4, 1, 2)
    gl = jnp.pad(tpad(gl, 3), ((0, 0),) * 4 + ((0, LANES - 3),)).reshape(B, H * rows, LANES)
    gb = jnp.broadcast_to(b_gate.reshape(3, H, G).transpose(1, 2, 0)[:, :, None, :], (H, G, Q_ROWS, 3))
    gb = jnp.pad(gb, ((0, 0),) * 3 + ((0, LANES - 3),)).reshape(H * rows, LANES)
    gain_rows = jnp.broadcast_to(gain.reshape(H, G, 1, hd), (H, G, Q_ROWS, hd)).reshape(H * rows, hd)
    new = nsa_new.reshape(B, T * ROWS_PER_TOKEN_NSA, hd)
    new = jnp.pad(new, ((0, 0), (0, (LANES - T) * ROWS_PER_TOKEN_NSA), (0, 0)))
    P = NSA_PAGES_PER_STEP
    assert n_pages % P == 0
    per_b = lambda *shape: pl.BlockSpec((None,) + shape, lambda b, j, pt: (b,) + (0,) * len(shape))
    const = lambda *shape: pl.BlockSpec(shape, lambda b, j, pt: (0,) * len(shape))
    page_spec = lambda r: pl.BlockSpec((None, None, page * ROWS_PER_TOKEN_NSA, hd),
                                       lambda b, j, pt: (layer, pt[b, j * P + r], 0, 0))
    out = pl.pallas_call(
        functools.partial(_nsa_decode_sweep_kernel, n_pages_step=P, page=page, past_len=past_len, sel_w=sel_w),
        grid_spec=pltpu.PrefetchScalarGridSpec(
            num_scalar_prefetch=1, grid=(B, n_pages // P),
            in_specs=[per_b(H * rows, hd), per_b(H, Q_ROWS, sel_w), per_b(LANES * ROWS_PER_TOKEN_NSA, hd),
                      per_b(H * rows, hd), per_b(H * rows, hd), per_b(H * rows, LANES),
                      const(H * rows, LANES), const(H * rows, hd)] + [page_spec(r) for r in range(P)],
            out_specs=per_b(H * rows, hd),
            scratch_shapes=[pltpu.VMEM((H * rows, 1), F32), pltpu.VMEM((H * rows, 1), F32),
                            pltpu.VMEM((H * rows, hd), F32)]),
        out_shape=jax.ShapeDtypeStruct((B, H * rows, hd), F32),
        compiler_params=_cparams(("parallel", "arbitrary")),
        name="nsa_decode_sweep",
    )(page_table, q_rows.reshape(B, H * rows, hd), sel, new, o_cmp.reshape(B, H * rows, hd),
      o_win.reshape(B, H * rows, hd), gl, gb, gain_rows, *([pages] * P))
    out = out.reshape(B, H, G, Q_ROWS, hd)[:, :, :, :T]
    return out.transpose(0, 3, 1, 2, 4).reshape(B, T, H * G * hd)


def _gate_weight(w_in, layer, n_main):
    D = w_in.shape[1]
    wg = w_in[layer, :, n_main:n_main + 3 * H_KV * G_NSA].reshape(D, 3, H_KV, G_NSA)
    wg = wg.transpose(0, 2, 1, 3).reshape(D, H_KV, 3 * G_NSA)
    return jnp.pad(wg, ((0, 0), (0, 0), (0, LANES - 3 * G_NSA))).reshape(D, H_KV * LANES)


def _regroup_gate_bias(b):
    bg = b.reshape(3, H_KV, G_NSA).transpose(1, 0, 2).reshape(H_KV, 3 * G_NSA)
    return jnp.pad(bg, ((0, 0), (0, LANES - 3 * G_NSA))).reshape(1, H_KV * LANES)


def _layer(l, x, mod, past, W, alpha, t_real):
    B, T, D = x.shape
    N = B * T
    n_main = 10240
    sh1, sc1, g1, sh2, sc2, g2 = [m[:, None, :] for m in jnp.split(mod, 6, axis=-1)]
    u = modulate(x, sc1, sh1, BF16).reshape(N, D)
    tm = 1024
    proj = matmul([u], W['w_in'], l, [0], n_main, tm=tm, tn=512, name="in_proj")
    gate_logits = matmul([u], _gate_weight(W['w_in'], l, n_main), None, [0], H_KV * LANES, tm=tm, tn=512,
                         name="gate_proj")
    bg = _regroup_gate_bias(W['b_gate'][l])
    proj3 = proj.reshape(B, T, n_main)[:, :t_real]
    if past is None:
        sb_rows = state_rows(proj, 1024, 2048).reshape(B, T, 2, H_SB, HEAD_DIM)
        nsa_rows = state_rows(proj, 7168, 2048).reshape(B, T, 4, H_KV, HEAD_DIM)
    else:
        sb_rows = proj3[:, :, 1024:3072].reshape(B, t_real, 2, H_SB, HEAD_DIM)
        nsa_rows = proj3[:, :, 7168:9216].reshape(B, t_real, 4, H_KV, HEAD_DIM)
    win_rows = proj3[:, :, 9216:10240].reshape(B, t_real, 2, H_KV, HEAD_DIM)
    sb_g = W['sb_norm_g'][l][None, :]
    nsa_g = W['nsa_norm_g'][l][None, :]
    conv_args = (W['conv_w'][l], W['conv_b'][l][None, :], W['conv_ln_g'][l][None, :], W['conv_ln_b'][l][None, :])
    if past is None:
        x_sb = sb_prompt(proj, sb_g, B, T)
        hist = jnp.zeros((B, HIST, D // 4), F32)
        x_conv, conv_state = conformer_conv(proj, hist, *conv_args, B, T, min(T, 256))
        kvc = compress_kv(proj, 56, W['cmp_w'], W['cmp_pe'], l, B, T)
        x_nsa = nsa_prompt(proj, kvc, gate_logits, bg, nsa_g, B, T)
        win_state = win_rows[:, -min(WINDOW, T):]
    else:
        cache_sb, nsa_pages, page_table, win_buf, conv_buf, past_len = past
        hist = jnp.pad(conv_buf, ((0, 0), (HIST - (CONV_W - 1), 0), (0, 0)))
        x_conv, conv_state = conformer_conv(proj, hist, *conv_args, B, T, t_real)
        assert past_len % CMP_STRIDE == 0 and t_real < CMP_STRIDE
        glog = gate_logits.reshape(B, T, H_KV, LANES)[:, :t_real, :, :3 * G_NSA].reshape(B, t_real, H_KV, 3, G_NSA)
        glog = glog.transpose(0, 1, 3, 2, 4).reshape(B, t_real, 3 * H_KV * G_NSA)
        win_all = jnp.concatenate([win_buf, win_rows], axis=1)
        win_state = win_all[:, -min(WINDOW, win_all.shape[1]):]
        q_n = proj3[:, :, 5120:7168].reshape(B, t_real, H_KV, G_NSA, HEAD_DIM)
        o_nsa = nsa_decode(q_n, nsa_rows, win_all, glog, W['b_gate'][l], W['nsa_norm_g'][l], nsa_pages, page_table,
                           W['cmp_w'], W['cmp_pe'], l, past_len)
        heads = lambda c0: proj3[:, :, c0:c0 + H_SB * HEAD_DIM].reshape(B, t_real, H_SB, HEAD_DIM)
        o_sb = sb_decode(heads(0), heads(1024), heads(2048), cache_sb, page_table, l, W['sb_norm_g'][l])
        pad_rows = lambda o: jnp.pad(o, ((0, 0), (0, T - t_real), (0, 0))).reshape(N, -1).astype(BF16)
        x_sb, x_nsa = pad_rows(o_sb), pad_rows(o_nsa)
    h = matmul([x_sb, x_conv, x_nsa], W['w_o'], l, [0, 1024, 2048], D, tm=tm, tn=512, name="out_proj")
    w_router = jnp.pad(jnp.concatenate([W['w_rg'][l], W['w_re'][l]], axis=1), ((0, 0), (0, LANES - 20)))
    b_router = jnp.pad(jnp.concatenate([W['b_rg'][l], W['b_re'][l]]), (0, LANES - 20))[None, :]
    x1, x2, rl = post_mixer(x, h.reshape(B, T, D), g1, W['ln1_g'][l][None, :], W['ln1_b'][l][None, :],
                            sc2, sh2, w_router, b_router, alpha)
    f = hier_moe(x2.reshape(N, D), rl.reshape(N, LANES), W['w_gate'], W['w_up'], W['w_down'], l, tm=tm)
    x_out = post_ffn(x1, f.reshape(B, T, D), g2, W['ln2_g'][l][None, :], W['ln2_b'][l][None, :], alpha)
    return x_out, (sb_rows, nsa_rows, win_state, conv_state)


def kernel(x_prompt, x_sample, cache_sb, cache_nsa, state_win, state_conv, page_table, c_prompt, c_sample, w_mod, b_mod, w_in, b_gate, conv_w, conv_b, conv_ln_g, conv_ln_b, cmp_w, cmp_pe, sb_norm_g, nsa_norm_g, w_o, ln1_g, ln1_b, w_rg, b_rg, w_re, b_re, w_gate, w_up, w_down, ln2_g, ln2_b):
    depth = w_mod.shape[0]
    alpha = (2.0 * depth) ** 0.25
    W = dict(w_in=w_in, b_gate=b_gate, conv_w=conv_w, conv_b=conv_b, conv_ln_g=conv_ln_g, conv_ln_b=conv_ln_b,
             cmp_w=cmp_w, cmp_pe=cmp_pe, sb_norm_g=sb_norm_g, nsa_norm_g=nsa_norm_g, w_o=w_o, ln1_g=ln1_g,
             ln1_b=ln1_b, w_rg=w_rg, b_rg=b_rg, w_re=w_re, b_re=b_re, w_gate=w_gate, w_up=w_up, w_down=w_down,
             ln2_g=ln2_g, ln2_b=ln2_b)
    past_len = page_table.shape[1] * cache_sb.shape[2]
    Bp, Bs = c_prompt.shape[0], c_sample.shape[0]
    c_all = jnp.concatenate([c_prompt, c_sample], axis=0)
    c_rows = -(-(Bp + Bs) // 16) * 16
    c_act = jnp.pad(jax.nn.silu(c_all), ((0, c_rows - Bp - Bs), (0, 0)))
    nsa_pages = cache_nsa.reshape(cache_nsa.shape[:2] + (-1, HEAD_DIM))
    t_s = x_sample.shape[1]
    t_pad =-(-t_s // SAMPLE_ROW_TILE) * SAMPLE_ROW_TILE
    xp = x_prompt
    xs = jnp.pad(x_sample, ((0, 0), (0, t_pad - t_s), (0, 0)))
    st_p, st_s = [], []
    for l in range(depth):
        mod = matmul([c_act], w_mod, l, [0], w_mod.shape[2], tm=c_rows, tn=512, bias=b_mod[l][None, :],
                     name="adaln_mod")
        xp, sp = _layer(l, xp, mod[:Bp], None, W, alpha, xp.shape[1])
        past = (cache_sb, nsa_pages, page_table, state_win[l], state_conv[l], past_len)
        xs, ss = _layer(l, xs, mod[Bp:Bp + Bs], past, W, alpha, t_s)
        st_p.append(sp)
        st_s.append(ss)
    sb_p, nsa_p, win_p, conv_p = [jnp.stack([s[i] for s in st_p]) for i in range(4)]
    sb_s, nsa_s, win_s, conv_s = [jnp.stack([s[i] for s in st_s]) for i in range(4)]
    return (xp, xs[:, :t_s], sb_p, nsa_p, win_p, conv_p, sb_s, nsa_s, win_s, conv_s)
```
